```python
import math
import jax, jax.numpy as jnp
from jax import lax
import numpy as np

D_MODEL = 1024
BATCH = 4
SEQ = 8192
DEPTH = 1

MLA_HEADS = 8
QK_NOPE_DIM = 64
QK_ROPE_DIM = 32
QK_HEAD_DIM = QK_NOPE_DIM + QK_ROPE_DIM
V_HEAD_DIM = 64
Q_LORA_RANK = 256
KV_LORA_RANK = 128
ROPE_THETA = 10000.0
Q_BLOCK = 128
SSM_WIDTH = 512
SSM_GROUP = 16
SSM_GROUPS = SSM_WIDTH // SSM_GROUP
SSM_STATE = 64
DT_MIN = 1e-3
DT_MAX = 1e-1
N_BRANCHES = 2
PEER_HEADS = 8
N_KEYS = 128
N_EXPERTS = N_KEYS * N_KEYS
PEER_TOPK = 16
PEER_QUERY_DIM = 128
PEER_HALF = PEER_QUERY_DIM // 2
PEER_CHUNK = 128
EPS = 1e-6

IN_SPLITS = (Q_LORA_RANK,
             Q_LORA_RANK + KV_LORA_RANK,
             Q_LORA_RANK + KV_LORA_RANK + QK_ROPE_DIM,
             Q_LORA_RANK + KV_LORA_RANK + QK_ROPE_DIM + SSM_WIDTH)
IN_DIM = IN_SPLITS[-1] + N_BRANCHES * D_MODEL

kernel_name = "hybrid_mla_s5_peer_encoder"


def rmsnorm(x, g):
    xf = x.astype(jnp.float32)
    y = xf * lax.rsqrt(jnp.mean(xf * xf, axis=-1, keepdims=True) + EPS)
    return (y * g.astype(jnp.float32)).astype(x.dtype)


def rope(t, cos, sin):
    t1, t2 = jnp.split(t.astype(jnp.float32), 2, axis=-1)
    out = jnp.concatenate([t1 * cos - t2 * sin, t1 * sin + t2 * cos], axis=-1)
    return out.astype(t.dtype)


def mla(h_q, h_kv, k_r, q_a_norm, w_q_b, kv_a_norm, w_kv_b, cos, sin):
    B, S, _ = h_q.shape
    q = (rmsnorm(h_q, q_a_norm) @ w_q_b).reshape(B, S, MLA_HEADS, QK_HEAD_DIM)
    q_nope, q_rope = q[..., :QK_NOPE_DIM], q[..., QK_NOPE_DIM:]
    q_rope = rope(q_rope, cos[:, None, :], sin[:, None, :])
    kv = (rmsnorm(h_kv, kv_a_norm) @ w_kv_b).reshape(B, S, MLA_HEADS, QK_NOPE_DIM + V_HEAD_DIM)
    k_nope, v = kv[..., :QK_NOPE_DIM], kv[..., QK_NOPE_DIM:]
    k_rope = rope(k_r, cos, sin)
    k_rope = jnp.broadcast_to(k_rope[:, :, None, :], (B, S, MLA_HEADS, QK_ROPE_DIM))
    q = jnp.concatenate([q_nope, q_rope], axis=-1)
    k = jnp.concatenate([k_nope, k_rope], axis=-1)
    scale = QK_HEAD_DIM ** -0.5
    nb = S // Q_BLOCK
    qb = q.reshape(B, nb, Q_BLOCK, MLA_HEADS, QK_HEAD_DIM).transpose(1, 0, 2, 3, 4)

    def attend(q_blk):
        s = jnp.einsum('bqhd,bkhd->bhqk', q_blk, k).astype(jnp.float32) * scale
        p = jax.nn.softmax(s, axis=-1)
        return jnp.einsum('bhqk,bkhd->bqhd', p.astype(v.dtype), v)

    o = lax.map(attend, qb)
    return o.transpose(1, 0, 2, 3, 4).reshape(B, S, MLA_HEADS * V_HEAD_DIM)


def s5_scan(u, lam_re, lam_im, log_dt, b_re, b_im, c_re, c_im, reverse):
    S = u.shape[1]
    lam = lax.complex(lam_re.astype(jnp.float32), lam_im.astype(jnp.float32))
    dt = jnp.exp(log_dt.astype(jnp.float32))[:, None]
    lam_bar = jnp.exp(lam * dt)
    b_c = lax.complex(b_re.astype(jnp.float32), b_im.astype(jnp.float32))
    b_bar = ((lam_bar - 1.0) / lam)[..., None] * b_c
    c_c = lax.complex(c_re.astype(jnp.float32), c_im.astype(jnp.float32))
    bu = jnp.einsum('bsgh,gph->bsgp', u.astype(jnp.complex64), b_bar)
    if reverse:
        bu = jnp.flip(bu, axis=1)
    a = jnp.broadcast_to(lam_bar[None, None], (1, S) + lam_bar.shape)

    def combine(e1, e2):
        a1, b1 = e1
        a2, b2 = e2
        return a2 * a1, a2 * b1 + b2

    _, states = lax.associative_scan(combine, (a, bu), axis=1)
    if reverse:
        states = jnp.flip(states, axis=1)
    return jnp.einsum('bsgp,ghp->bsgh', states, c_c).real


def s5_bidir(u, lam_re, lam_im, log_dt, b_re, b_im, c_re, c_im, d_skip):
    B, S, _ = u.shape
    uf = u.astype(jnp.float32).reshape(B, S, SSM_GROUPS, SSM_GROUP)
    y = d_skip.astype(jnp.float32).reshape(SSM_GROUPS, SSM_GROUP) * uf
    for d in range(2):
        y = y + s5_scan(uf, lam_re[d], lam_im[d], log_dt[d], b_re[d], b_im[d],
                        c_re[d], c_im[d], reverse=(d == 1))
    return y.reshape(B, S, SSM_WIDTH).astype(u.dtype)


def peer(h, w_query, sub_keys, w_down, w_up):
    B, S, D = h.shape
    T = B * S
    hf = h.reshape(T, D)
    q = (hf @ w_query).reshape(T, PEER_HEADS, 2, PEER_HALF)
    scores = jnp.einsum('thnd,hnkd->thnk', q, sub_keys).astype(jnp.float32)
    s_top, i_top = lax.top_k(scores, PEER_TOPK)
    cand = s_top[:, :, 0, :, None] + s_top[:, :, 1, None, :]
    cand_idx = i_top[:, :, 0, :, None] * N_KEYS + i_top[:, :, 1, None, :]
    cand = cand.reshape(T, PEER_HEADS, PEER_TOPK * PEER_TOPK)
    cand_idx = cand_idx.reshape(T, PEER_HEADS, PEER_TOPK * PEER_TOPK)
    best, pos = lax.top_k(cand, PEER_TOPK)
    idx = jnp.take_along_axis(cand_idx, pos, axis=-1)
    gate = jax.nn.softmax(best, axis=-1)
    nc = T // PEER_CHUNK
    xc = hf.reshape(nc, PEER_CHUNK, D)
    ic = idx.reshape(nc, PEER_CHUNK, PEER_HEADS, PEER_TOPK)
    gc = gate.reshape(nc, PEER_CHUNK, PEER_HEADS, PEER_TOPK)

    def expert_chunk(args):
        xb, ib, gb = args
        u = w_down[ib]
        act = jax.nn.gelu(jnp.einsum('cd,chkd->chk', xb, u).astype(jnp.float32)) * gb
        v = w_up[ib]
        return jnp.einsum('chk,chkd->cd', act.astype(v.dtype), v)

    y = lax.map(expert_chunk, (xc, ic, gc))
    return y.reshape(B, S, D)


def setup_inputs(seed: int = 0) -> dict:
    key = jax.random.key(seed)
    ks = jax.random.split(key, 32)
    f32 = jnp.float32
    nrm = lambda k, shape, s: jax.random.normal(k, shape, f32) * s
    L, G, P, Hg = DEPTH, SSM_GROUPS, SSM_STATE, SSM_GROUP
    lam_im_base = jnp.pi * jnp.arange(P, dtype=f32)
    return {
        "x": nrm(ks[0], (BATCH, SEQ, D_MODEL), 1.0),
        "norm_mix": 1.0 + nrm(ks[1], (L, D_MODEL), 0.02),
        "w_in": nrm(ks[2], (L, D_MODEL, IN_DIM), D_MODEL ** -0.5),
        "q_a_norm": 1.0 + nrm(ks[3], (L, Q_LORA_RANK), 0.02),
        "w_q_b": nrm(ks[4], (L, Q_LORA_RANK, MLA_HEADS * QK_HEAD_DIM), Q_LORA_RANK ** -0.5),
        "kv_a_norm": 1.0 + nrm(ks[5], (L, KV_LORA_RANK), 0.02),
        "w_kv_b": nrm(ks[6], (L, KV_LORA_RANK, MLA_HEADS * (QK_NOPE_DIM + V_HEAD_DIM)), KV_LORA_RANK ** -0.5),
        "w_o_attn": nrm(ks[7], (L, MLA_HEADS * V_HEAD_DIM, D_MODEL), (MLA_HEADS * V_HEAD_DIM) ** -0.5),
        "lam_re": -0.5 * jnp.exp(nrm(ks[8], (L, 2, G, P), 0.05)),
        "lam_im": lam_im_base + nrm(ks[9], (L, 2, G, P), 0.01),
        "log_dt": jax.random.uniform(ks[10], (L, 2, G), f32, math.log(DT_MIN), math.log(DT_MAX)),
        "b_re": nrm(ks[11], (L, 2, G, P, Hg), (2 * Hg) ** -0.5),
        "b_im": nrm(ks[12], (L, 2, G, P, Hg), (2 * Hg) ** -0.5),
        "c_re": nrm(ks[13], (L, 2, G, Hg, P), 0.5),
        "c_im": nrm(ks[14], (L, 2, G, Hg, P), 0.5),
        "d_skip": nrm(ks[15], (L, SSM_WIDTH), 0.5),
        "w_glu": nrm(ks[16], (L, SSM_WIDTH, 2 * SSM_WIDTH), SSM_WIDTH ** -0.5),
        "w_o_ssm": nrm(ks[17], (L, SSM_WIDTH, D_MODEL), SSM_WIDTH ** -0.5),
        "w_out": nrm(ks[18], (L, D_MODEL, D_MODEL), D_MODEL ** -0.5),
        "norm_ffn": 1.0 + nrm(ks[19], (L, D_MODEL), 0.02),
        "w_query": nrm(ks[20], (L, D_MODEL, PEER_HEADS * PEER_QUERY_DIM), D_MODEL ** -0.5),
        "sub_keys": nrm(ks[21], (L, PEER_HEADS, 2, N_KEYS, PEER_HALF), PEER_HALF ** -0.5),
        "w_down": nrm(ks[22], (L, N_EXPERTS, D_MODEL), D_MODEL ** -0.5),
        "w_up": nrm(ks[23], (L, N_EXPERTS, D_MODEL), PEER_HEADS ** -0.5),
        "final_norm": 1.0 + nrm(ks[24], (D_MODEL,), 0.02),
    }


def reference(x, norm_mix, w_in, q_a_norm, w_q_b, kv_a_norm, w_kv_b, w_o_attn,
              lam_re, lam_im, log_dt, b_re, b_im, c_re, c_im, d_skip, w_glu, w_o_ssm,
              w_out, norm_ffn, w_query, sub_keys, w_down, w_up, final_norm):
    B, S, D = x.shape
    pos = jnp.arange(S, dtype=jnp.float32)
    inv_freq = 1.0 / (ROPE_THETA ** (jnp.arange(0, QK_ROPE_DIM, 2, dtype=jnp.float32) / QK_ROPE_DIM))
    ang = pos[:, None] * inv_freq[None, :]
    cos, sin = jnp.cos(ang), jnp.sin(ang)
    for l in range(DEPTH):
        h = rmsnorm(x, norm_mix[l])
        proj = h @ w_in[l]
        h_q, h_kv, k_r, u_ssm, gates = jnp.split(proj, IN_SPLITS, axis=-1)
        y_attn = mla(h_q, h_kv, k_r, q_a_norm[l], w_q_b[l], kv_a_norm[l], w_kv_b[l], cos, sin) @ w_o_attn[l]
        y_s = s5_bidir(u_ssm, lam_re[l], lam_im[l], log_dt[l], b_re[l], b_im[l],
                       c_re[l], c_im[l], d_skip[l])
        z = jax.nn.gelu(y_s) @ w_glu[l]
        y_ssm = (z[..., :SSM_WIDTH] * jax.nn.sigmoid(z[..., SSM_WIDTH:])) @ w_o_ssm[l]
        g = jax.nn.sigmoid(gates.astype(jnp.float32)).astype(x.dtype).reshape(B, S, N_BRANCHES, D)
        mixed = g[:, :, 0, :] * y_attn + g[:, :, 1, :] * y_ssm
        x = x + mixed @ w_out[l]
        x = x + peer(rmsnorm(x, norm_ffn[l]), w_query[l], sub_keys[l], w_down[l], w_up[l])
    return rmsnorm(x, final_norm)
```

```python
import functools
import math

import jax
import jax.numpy as jnp
from jax import lax
from jax.experimental import pallas as pl
from jax.experimental.pallas import tpu as pltpu

MLA_HEADS = 8
QK_NOPE_DIM = 64
QK_ROPE_DIM = 32
QK_HEAD_DIM = QK_NOPE_DIM + QK_ROPE_DIM
V_HEAD_DIM = 64
Q_LORA_RANK = 256
KV_LORA_RANK = 128
ROPE_THETA = 10000.0
SSM_WIDTH = 512
SSM_GROUP = 16
SSM_GROUPS = SSM_WIDTH // SSM_GROUP
SSM_STATE = 64
PEER_HEADS = 8
N_KEYS = 128
PEER_TOPK = 16
PEER_HALF = 64
EPS = 1e-6

HEAD_PAD = 128
SSM_CHUNK = 64
MXU_DTYPE = jnp.bfloat16
VMEM_LIMIT_BYTES = 48 * 1024 * 1024

_NT = (((1,), (1,)), ((), ()))
_TN = (((0,), (0,)), ((), ()))


def _dot(a, b):
    return jnp.dot(a, b, preferred_element_type=jnp.float32)


def _rms(x, g):
    return x * lax.rsqrt(jnp.mean(x * x, axis=-1, keepdims=True) + EPS) * g


def _params(*sem):
    return pltpu.CompilerParams(dimension_semantics=sem, vmem_limit_bytes=VMEM_LIMIT_BYTES)


def _full(shape):
    return pl.BlockSpec(shape, lambda *_: (0,) * len(shape))


def _inproj_kernel(x_ref, cos_ref, sin_ref, gmix_ref, wcat_ref, gq_ref, wq_ref, gkv_ref, wkv_ref,
                   q_ref, k_ref, v_ref, u_ref, gate_ref):
    x = x_ref[0]
    h = _rms(x, gmix_ref[...]).astype(MXU_DTYPE)
    cos = cos_ref[...]
    sin = sin_ref[...]
    o = 0
    hq = _dot(h, wcat_ref[:, o:o + Q_LORA_RANK]); o += Q_LORA_RANK
    hkv = _dot(h, wcat_ref[:, o:o + KV_LORA_RANK]); o += KV_LORA_RANK
    kr = _dot(h, wcat_ref[:, o:o + 2 * HEAD_PAD]); o += 2 * HEAD_PAD
    u_ref[...] = _dot(h, wcat_ref[:, o:o + SSM_WIDTH]).astype(u_ref.dtype); o += SSM_WIDTH
    gate_ref[...] = jax.nn.sigmoid(_dot(h, wcat_ref[:, o:])).astype(gate_ref.dtype)

    nq = MLA_HEADS * HEAD_PAD
    hqn = _rms(hq, gq_ref[...]).astype(MXU_DTYPE)
    qq = _dot(hqn, wq_ref[...])
    hkvn = _rms(hkv, gkv_ref[...]).astype(MXU_DTYPE)
    kv = _dot(hkvn, wkv_ref[...])
    k_rope = kr[:, :HEAD_PAD] * cos + kr[:, HEAD_PAD:] * sin
    scale = QK_HEAD_DIM ** -0.5
    lane = lax.broadcasted_iota(jnp.int32, (1, HEAD_PAD), 1)
    ones_pad = (lane >= V_HEAD_DIM).astype(jnp.float32)
    for hd in range(MLA_HEADS):
        sl = slice(hd * HEAD_PAD, (hd + 1) * HEAD_PAD)
        q = qq[:, sl] * cos + qq[:, nq + hd * HEAD_PAD: nq + (hd + 1) * HEAD_PAD] * sin
        q_ref[0, hd] = (q * scale).astype(q_ref.dtype)
        k_ref[0, hd] = (kv[:, sl] + k_rope).astype(k_ref.dtype)
        v_ref[0, hd] = (kv[:, nq + hd * HEAD_PAD: nq + (hd + 1) * HEAD_PAD] + ones_pad).astype(v_ref.dtype)


def _inproj(x, cos_t, sin_t, gmix, wcat, gq, wq, gkv, wkv, tm):
    B, S, D = x.shape
    H = MLA_HEADS
    ncat = wcat.shape[1]
    ngate = ncat - (Q_LORA_RANK + KV_LORA_RANK + 2 * HEAD_PAD + SSM_WIDTH)
    grid = (B, S // tm)
    hs = jax.ShapeDtypeStruct((B, H, S, HEAD_PAD), MXU_DTYPE)
    head_spec = pl.BlockSpec((1, H, tm, HEAD_PAD), lambda b, s: (b, 0, s, 0))
    return pl.pallas_call(
        _inproj_kernel,
        grid=grid,
        in_specs=[
            pl.BlockSpec((1, tm, D), lambda b, s: (b, s, 0)),
            pl.BlockSpec((tm, HEAD_PAD), lambda b, s: (s, 0)),
            pl.BlockSpec((tm, HEAD_PAD), lambda b, s: (s, 0)),
            _full(gmix.shape), _full(wcat.shape), _full(gq.shape), _full(wq.shape),
            _full(gkv.shape), _full(wkv.shape),
        ],
        out_specs=[
            head_spec, head_spec, head_spec,
            pl.BlockSpec((tm, SSM_WIDTH), lambda b, s, n=S // tm: (b * n + s, 0)),
            pl.BlockSpec((tm, ngate), lambda b, s, n=S // tm: (b * n + s, 0)),
        ],
        out_shape=[hs, hs, hs,
                   jax.ShapeDtypeStruct((B * S, SSM_WIDTH), MXU_DTYPE),
                   jax.ShapeDtypeStruct((B * S, ngate), MXU_DTYPE)],
        compiler_params=_params("parallel", "parallel"),
        name="inproj",
    )(x, cos_t, sin_t, gmix, wcat, gq, wq, gkv, wkv)


def _attn_kernel(q_ref, k_ref, v_ref, o_ref, m_ref, acc_ref):
    ki = pl.program_id(3)

    @pl.when(ki == 0)
    def _():
        m_ref[...] = jnp.full(m_ref.shape, -jnp.inf, jnp.float32)
        acc_ref[...] = jnp.zeros(acc_ref.shape, jnp.float32)

    tk = k_ref.shape[2]
    for j in range(2):
        s = lax.dot_general(q_ref[0, j], k_ref[0, j], _NT, preferred_element_type=jnp.float32)
        m_prev = m_ref[j]
        m_new = jnp.maximum(m_prev, jnp.max(s, axis=1, keepdims=True))
        p = jnp.exp(s - jnp.tile(m_new, (1, tk // HEAD_PAD)))
        alpha = jnp.exp(m_prev - m_new)
        acc_ref[j] = alpha * acc_ref[j] + _dot(p.astype(MXU_DTYPE), v_ref[0, j])
        m_ref[j] = m_new

    @pl.when(ki == pl.num_programs(3) - 1)
    def _():
        a0, a1 = acc_ref[0], acc_ref[1]
        lane = lax.broadcasted_iota(jnp.int32, a0.shape, 1)
        o0 = a0 / pltpu.roll(a0, V_HEAD_DIM, 1)
        o1 = pltpu.roll(a1, V_HEAD_DIM, 1) / a1
        o_ref[0] = jnp.where(lane < V_HEAD_DIM, o0, o1).astype(o_ref.dtype)


def _attention(q, k, v, tq, tk):
    B, H, S, _ = q.shape
    grid = (B, H // 2, S // tq, S // tk)
    return pl.pallas_call(
        _attn_kernel,
        grid=grid,
        in_specs=[
            pl.BlockSpec((1, 2, tq, HEAD_PAD), lambda b, h, i, j: (b, h, i, 0)),
            pl.BlockSpec((1, 2, tk, HEAD_PAD), lambda b, h, i, j: (b, h, j, 0)),
            pl.BlockSpec((1, 2, tk, HEAD_PAD), lambda b, h, i, j: (b, h, j, 0)),
        ],
        out_specs=pl.BlockSpec((1, tq, 2 * V_HEAD_DIM), lambda b, h, i, j: (b, i, h)),
        out_shape=jax.ShapeDtypeStruct((B, S, H * V_HEAD_DIM), MXU_DTYPE),
        scratch_shapes=[pltpu.VMEM((2, tq, HEAD_PAD), jnp.float32),
                        pltpu.VMEM((2, tq, HEAD_PAD), jnp.float32)],
        compiler_params=_params("parallel", "parallel", "parallel", "arbitrary"),
        name="attention",
    )(q, k, v)


def _s5_kernel(u_ref, mt_ref, win_ref, wout_ref, dec_ref, y_ref, *, chunks_per_seq):
    u = u_ref[0]
    y = _dot(u, mt_ref[0])
    loc = _dot(u, win_ref[0])
    nc = u.shape[0]
    half = SSM_STATE
    cidx = lax.broadcasted_iota(jnp.int32, (nc, 2 * half), 0) % chunks_per_seq
    xf = loc[:, :2 * half]
    xb = loc[:, 2 * half:]
    levels = chunks_per_seq.bit_length() - 1
    for lv in range(levels):
        sh = 1 << lv
        prev = jnp.where(cidx >= sh, pltpu.roll(xf, sh, 0), 0.0)
        xf = xf + prev * dec_ref[0, 4 * lv + 0:4 * lv + 1, :] \
            + pltpu.roll(prev, half, 1) * dec_ref[0, 4 * lv + 1:4 * lv + 2, :]
        nxt = jnp.where(cidx < chunks_per_seq - sh, pltpu.roll(xb, nc - sh, 0), 0.0)
        xb = xb + nxt * dec_ref[0, 4 * lv + 2:4 * lv + 3, :] \
            + pltpu.roll(nxt, half, 1) * dec_ref[0, 4 * lv + 3:4 * lv + 4, :]
    xin_f = jnp.where(cidx >= 1, pltpu.roll(xf, 1, 0), 0.0)
    xin_b = jnp.where(cidx < chunks_per_seq - 1, pltpu.roll(xb, nc - 1, 0), 0.0)
    xin = jnp.concatenate([xin_f, xin_b], axis=1).astype(MXU_DTYPE)
    y = y + _dot(xin, wout_ref[0])
    y_ref[0] = jax.nn.gelu(y).astype(y_ref.dtype)


def _s5(u_g, mt, win, wout, dec, chunks_per_seq):
    G, NC, W = u_g.shape
    kern = functools.partial(_s5_kernel, chunks_per_seq=chunks_per_seq)
    blk = lambda a: pl.BlockSpec((1,) + a.shape[1:], lambda g: (g, 0, 0))
    return pl.pallas_call(
        kern,
        grid=(G,),
        in_specs=[blk(u_g), blk(mt), blk(win), blk(wout), blk(dec)],
        out_specs=pl.BlockSpec((1, NC, W), lambda g: (g, 0, 0)),
        out_shape=jax.ShapeDtypeStruct((G, NC, W), MXU_DTYPE),
        compiler_params=_params("parallel"),
        name="s5_scan",
    )(u_g, mt, win, wout, dec)


def _s5_operators(lam_re, lam_im, log_dt, b_re, b_im, c_re, c_im, d_skip, chunks_per_seq):
    L, G, P, Hg = SSM_CHUNK, SSM_GROUPS, SSM_STATE, SSM_GROUP
    f32 = jnp.float32
    lam = lax.complex(lam_re.astype(f32), lam_im.astype(f32))
    dt = jnp.exp(log_dt.astype(f32))[..., None]
    lam_dt = lam * dt
    lam_bar = jnp.exp(lam_dt)
    b_bar = ((lam_bar - 1.0) / lam)[..., None] * lax.complex(b_re.astype(f32), b_im.astype(f32))
    c_c = lax.complex(c_re.astype(f32), c_im.astype(f32))
    steps = jnp.arange(L + 1, dtype=f32)
    apow = jnp.exp(lam_dt[:, :, None, :] * steps[None, None, :, None])
    kern = jnp.einsum('xgip,xgdp,xgpj->xgdij', c_c, apow[:, :, :L], b_bar).real
    t = jnp.arange(L)
    lag = t[None, :] - t[:, None]
    kf = jnp.where((lag >= 0)[None, :, :, None, None], kern[0][:, jnp.clip(lag, 0, L - 1)], 0.0)
    kb = jnp.where((lag <= 0)[None, :, :, None, None], kern[1][:, jnp.clip(-lag, 0, L - 1)], 0.0)
    skip = (lag == 0)[None, :, :, None, None] * (
        jnp.eye(Hg, dtype=f32)[None, None, None] * d_skip.astype(f32).reshape(G, 1, 1, Hg, 1))
    mt = (kf + kb + skip).transpose(0, 1, 4, 2, 3).reshape(G, L * Hg, L * Hg)
    wf = apow[0][:, ::-1][:, 1:, :, None] * b_bar[0][:, None]
    wb = apow[1][:, :L, :, None] * b_bar[1][:, None]
    to_in = lambda w: jnp.concatenate([w.real, w.imag], axis=2).transpose(0, 1, 3, 2).reshape(G, L * Hg, 2 * P)
    win = jnp.concatenate([to_in(wf), to_in(wb)], axis=2)
    of = c_c[0][:, None] * apow[0][:, 1:, None, :]
    ob = c_c[1][:, None] * apow[1][:, ::-1][:, :L, None, :]
    to_out = lambda w: jnp.concatenate([w.real, -w.imag], axis=3).transpose(0, 3, 1, 2).reshape(G, 2 * P, L * Hg)
    wout = jnp.concatenate([to_out(of), to_out(ob)], axis=1)
    levels = max(chunks_per_seq.bit_length() - 1, 1)
    mult = (L * (2.0 ** jnp.arange(levels, dtype=f32)))
    dec_c = jnp.exp(lam_dt[:, :, None, :] * mult[None, None, :, None])
    rows = jnp.stack([jnp.concatenate([dec_c.real, dec_c.real], -1),
                      jnp.concatenate([-dec_c.imag, dec_c.imag], -1)], axis=3)
    dec = rows.transpose(1, 2, 0, 3, 4).reshape(G, levels * 4, 2 * P)
    return mt.astype(MXU_DTYPE), win.astype(MXU_DTYPE), wout.astype(MXU_DTYPE), dec.astype(f32)


def _mix_kernel(x_ref, o_ref, gy_ref, gate_ref, woa_ref, wglu_ref, wos_ref, wout_ref, gffn_ref,
                x1_ref, h2_ref):
    D = x_ref.shape[1]
    ya = _dot(o_ref[...], woa_ref[...])
    z = _dot(gy_ref[...], wglu_ref[...])
    yg = z[:, :SSM_WIDTH] * jax.nn.sigmoid(z[:, SSM_WIDTH:])
    ys = _dot(yg.astype(MXU_DTYPE), wos_ref[...])
    g = gate_ref[...].astype(jnp.float32)
    mixed = g[:, :D] * ya + g[:, D:] * ys
    x1 = x_ref[...] + _dot(mixed.astype(MXU_DTYPE), wout_ref[...])
    x1_ref[...] = x1
    h2_ref[...] = _rms(x1, gffn_ref[...]).astype(h2_ref.dtype)


def _mix(x2, o2, gy, gates, woa, wglu, wos, wout, gffn, tm):
    T, D = x2.shape
    row = lambda a: pl.BlockSpec((tm, a.shape[1]), lambda i: (i, 0))
    return pl.pallas_call(
        _mix_kernel,
        grid=(T // tm,),
        in_specs=[row(x2), row(o2), row(gy), row(gates),
                  _full(woa.shape), _full(wglu.shape), _full(wos.shape), _full(wout.shape), _full(gffn.shape)],
        out_specs=[pl.BlockSpec((tm, D), lambda i: (i, 0)), pl.BlockSpec((tm, D), lambda i: (i, 0))],
        out_shape=[jax.ShapeDtypeStruct((T, D), jnp.float32), jax.ShapeDtypeStruct((T, D), MXU_DTYPE)],
        compiler_params=_params("parallel"),
        name="mix",
    )(x2, o2, gy, gates, woa, wglu, wos, wout, gffn)


def _kth_largest_rows(s, k, out_ref=None):
    kth = None
    for i in range(k):
        kth = jnp.max(s, axis=0, keepdims=True)
        if out_ref is not None:
            out_ref[i:i + 1, :] = kth
        s = jnp.where(s == kth, -jnp.inf, s)
    return kth, jnp.max(s, axis=0, keepdims=True)


def _peer_stats_kernel(h2_ref, wqt_ref, keys_ref, thr_ref, c_ref, e2_ref, s2_ref,
                       top1_ref, top2_ref, cand_ref):
    K = PEER_TOPK
    qpt = lax.dot_general(wqt_ref[...], h2_ref[...], _NT, preferred_element_type=jnp.float32)
    for hd in range(PEER_HEADS):
        r0 = hd * 2 * PEER_HALF
        s1 = _dot(keys_ref[2 * hd], qpt[r0:r0 + PEER_HALF].astype(MXU_DTYPE))
        s2 = _dot(keys_ref[2 * hd + 1], qpt[r0 + PEER_HALF:r0 + 2 * PEER_HALF].astype(MXU_DTYPE))
        _kth_largest_rows(s1, K, top1_ref)
        _kth_largest_rows(s2, K, top2_ref)
        top2 = top2_ref[...]
        for i in range(K):
            cand_ref[i * K:(i + 1) * K, :] = top1_ref[i:i + 1, :] + top2
        cand = cand_ref[...]
        tau, below = _kth_largest_rows(cand, K)
        m1 = top1_ref[0:1, :]
        m2 = top2_ref[0:1, :]
        z = jnp.sum(jnp.where(cand >= tau, jnp.exp(cand - (m1 + m2)), 0.0), axis=0, keepdims=True)
        in1 = s1 >= top1_ref[K - 1:K, :]
        in2 = s2 >= top2_ref[K - 1:K, :]
        thr_ref[hd] = jnp.where(in1, 0.5 * (tau + below) - s1, jnp.inf)
        c_ref[hd] = jnp.exp(s1 - m1) / z
        e2_ref[hd] = jnp.exp(s2 - m2)
        s2_ref[hd] = jnp.where(in2, s2, -jnp.inf)


def _peer_stats(h2, wqt, keys, tm):
    T, D = h2.shape
    st = jax.ShapeDtypeStruct((PEER_HEADS, N_KEYS, T), jnp.float32)
    st_spec = pl.BlockSpec((PEER_HEADS, N_KEYS, tm), lambda i: (0, 0, i))
    return pl.pallas_call(
        _peer_stats_kernel,
        grid=(T // tm,),
        in_specs=[pl.BlockSpec((tm, D), lambda i: (i, 0)), _full(wqt.shape), _full(keys.shape)],
        out_specs=[st_spec] * 4,
        out_shape=[st] * 4,
        scratch_shapes=[pltpu.VMEM((PEER_TOPK, tm), jnp.float32),
                        pltpu.VMEM((PEER_TOPK, tm), jnp.float32),
                        pltpu.VMEM((PEER_TOPK * PEER_TOPK, tm), jnp.float32)],
        compiler_params=_params("parallel"),
        name="peer_stats",
    )(h2, wqt, keys)


def _peer_kernel(h2_ref, wd_ref, wu_ref, thr_ref, c_ref, e2_ref, s2_ref, x1_ref, gfin_ref,
                 out_ref, acc_ref, act_ref):
    e = pl.program_id(1)

    @pl.when(e == 0)
    def _():
        acc_ref[...] = jnp.zeros(acc_ref.shape, jnp.float32)

    te = wd_ref.shape[0]
    a_t = lax.dot_general(wd_ref[...], h2_ref[...], _NT, preferred_element_type=jnp.float32)
    for j in range(te // N_KEYS):
        i1 = e * (te // N_KEYS) + j
        gate = None
        for hd in range(PEER_HEADS):
            thr = thr_ref[hd, pl.ds(i1, 1), :]
            cw = c_ref[hd, pl.ds(i1, 1), :]
            g = jnp.where(s2_ref[hd] >= thr, e2_ref[hd], 0.0) * cw
            gate = g if gate is None else gate + g
        blk = a_t[j * N_KEYS:(j + 1) * N_KEYS]
        act_ref[j * N_KEYS:(j + 1) * N_KEYS, :] = (jax.nn.gelu(blk) * gate).astype(act_ref.dtype)
    acc_ref[...] += lax.dot_general(act_ref[...], wu_ref[...], _TN, preferred_element_type=jnp.float32)

    @pl.when(e == pl.num_programs(1) - 1)
    def _():
        out_ref[...] = _rms(x1_ref[...] + acc_ref[...], gfin_ref[...])


def _peer(h2, wd, wu, thr, cw, e2, s2, x1, gfin, tm, te):
    T, D = h2.shape
    E = wd.shape[0]
    st_spec = pl.BlockSpec((PEER_HEADS, N_KEYS, tm), lambda i, e: (0, 0, i))
    return pl.pallas_call(
        _peer_kernel,
        grid=(T // tm, E // te),
        in_specs=[pl.BlockSpec((tm, D), lambda i, e: (i, 0)),
                  pl.BlockSpec((te, D), lambda i, e: (e, 0)),
                  pl.BlockSpec((te, D), lambda i, e: (e, 0)),
                  st_spec, st_spec, st_spec, st_spec,
                  pl.BlockSpec((tm, D), lambda i, e: (i, 0)),
                  pl.BlockSpec(gfin.shape, lambda i, e: (0, 0))],
        out_specs=pl.BlockSpec((tm, D), lambda i, e: (i, 0)),
        out_shape=jax.ShapeDtypeStruct((T, D), jnp.float32),
        scratch_shapes=[pltpu.VMEM((tm, D), jnp.float32), pltpu.VMEM((te, tm), MXU_DTYPE)],
        compiler_params=_params("parallel", "arbitrary"),
        name="peer_experts",
    )(h2, wd, wu, thr, cw, e2, s2, x1, gfin)


def _head_cols(w, per_head, start, width, dst):
    K = w.shape[0]
    blk = w.reshape(K, MLA_HEADS, per_head)[:, :, start:start + width]
    blk = jnp.pad(blk, ((0, 0), (0, 0), (dst, HEAD_PAD - dst - width)))
    return blk.reshape(K, MLA_HEADS * HEAD_PAD)


def _tile(n, pref):
    return pref if n % pref == 0 else n


def kernel(x, norm_mix, w_in, q_a_norm, w_q_b, kv_a_norm, w_kv_b, w_o_attn, lam_re, lam_im, log_dt, b_re, b_im, c_re, c_im, d_skip, w_glu, w_o_ssm, w_out, norm_ffn, w_query, sub_keys, w_down, w_up, final_norm):
    B, S, D = x.shape
    T = B * S
    f32 = jnp.float32
    cd = MXU_DTYPE
    half = QK_ROPE_DIM // 2
    l = 0

    pos = jnp.arange(S, dtype=f32)
    inv_freq = 1.0 / (ROPE_THETA ** (jnp.arange(0, QK_ROPE_DIM, 2, dtype=f32) / QK_ROPE_DIM))
    ang = pos[:, None] * inv_freq[None, :]
    cos, sin = jnp.cos(ang), jnp.sin(ang)
    pad = HEAD_PAD - QK_HEAD_DIM
    cos_t = jnp.concatenate([jnp.ones((S, QK_NOPE_DIM), f32), cos, cos, jnp.zeros((S, pad), f32)], axis=1)
    sin_t = jnp.concatenate([jnp.zeros((S, QK_NOPE_DIM), f32), -sin, sin, jnp.zeros((S, pad), f32)], axis=1)

    w = w_in[l]
    c0, c1, c2, c3 = Q_LORA_RANK, Q_LORA_RANK + KV_LORA_RANK, Q_LORA_RANK + KV_LORA_RANK + QK_ROPE_DIM, \
        Q_LORA_RANK + KV_LORA_RANK + QK_ROPE_DIM + SSM_WIDTH
    w_kr = w[:, c1:c2]
    zl = jnp.zeros((D, QK_NOPE_DIM), f32)
    zr = jnp.zeros((D, pad), f32)
    kr_plain = jnp.concatenate([zl, w_kr, zr], axis=1)
    kr_swap = jnp.concatenate([zl, w_kr[:, half:], w_kr[:, :half], zr], axis=1)
    wcat = jnp.concatenate([w[:, :c0], w[:, c0:c1], kr_plain, kr_swap, w[:, c2:c3], w[:, c3:]], axis=1).astype(cd)

    wq = w_q_b[l]
    wq_plain = _head_cols(wq, QK_HEAD_DIM, 0, QK_HEAD_DIM, 0)
    wq_swap = (_head_cols(wq, QK_HEAD_DIM, QK_NOPE_DIM + half, half, QK_NOPE_DIM)
               + _head_cols(wq, QK_HEAD_DIM, QK_NOPE_DIM, half, QK_NOPE_DIM + half))
    wq_cat = jnp.concatenate([wq_plain, wq_swap], axis=1).astype(cd)
    wkv = w_kv_b[l]
    wkv_cat = jnp.concatenate([_head_cols(wkv, QK_NOPE_DIM + V_HEAD_DIM, 0, QK_NOPE_DIM, 0),
                               _head_cols(wkv, QK_NOPE_DIM + V_HEAD_DIM, QK_NOPE_DIM, V_HEAD_DIM, 0)],
                              axis=1).astype(cd)

    tm = _tile(S, 256)
    q, k, v, u, gates = _inproj(x, cos_t, sin_t, norm_mix[l][None], wcat, q_a_norm[l][None], wq_cat,
                                kv_a_norm[l][None], wkv_cat, tm)

    o = _attention(q, k, v, _tile(S, 512), _tile(S, 512))

    L, G, Hg = SSM_CHUNK, SSM_GROUPS, SSM_GROUP
    ncs = S // L
    mt, win, wout_s, dec = _s5_operators(lam_re[l], lam_im[l], log_dt[l], b_re[l], b_im[l], c_re[l], c_im[l],
                                         d_skip[l], ncs)
    u_g = u.reshape(B, ncs, L, G, Hg).transpose(3, 0, 1, 2, 4).reshape(G, B * ncs, L * Hg)
    gy_g = _s5(u_g, mt, win, wout_s, dec, ncs)
    gy = gy_g.reshape(G, B, ncs, L, Hg).transpose(1, 2, 3, 0, 4).reshape(T, SSM_WIDTH)

    x1, h2 = _mix(x.reshape(T, D), o.reshape(T, MLA_HEADS * V_HEAD_DIM), gy, gates,
                  w_o_attn[l].astype(cd), w_glu[l].astype(cd), w_o_ssm[l].astype(cd), w_out[l].astype(cd),
                  norm_ffn[l][None], _tile(T, 512))

    keys = sub_keys[l].reshape(PEER_HEADS * 2, N_KEYS, PEER_HALF).astype(cd)
    thr, cw, e2, s2 = _peer_stats(h2, w_query[l].T.astype(cd), keys, _tile(T, 256))
    out = _peer(h2, w_down[l].astype(cd), w_up[l].astype(cd), thr, cw, e2, s2, x1, final_norm[None],
                _tile(T, 512), 512)
    return out.reshape(B, S, D)
```

```python
import functools
import math

import jax
import jax.numpy as jnp
from jax import lax
from jax.experimental import pallas as pl
from jax.experimental.pallas import tpu as pltpu

MLA_HEADS = 8
QK_NOPE_DIM = 64
QK_ROPE_DIM = 32
QK_HEAD_DIM = QK_NOPE_DIM + QK_ROPE_DIM
V_HEAD_DIM = 64
Q_LORA_RANK = 256
KV_LORA_RANK = 128
ROPE_THETA = 10000.0
SSM_WIDTH = 512
SSM_GROUP = 16
SSM_GROUPS = SSM_WIDTH // SSM_GROUP
SSM_STATE = 64
PEER_HEADS = 8
N_KEYS = 128
PEER_TOPK = 16
PEER_HALF = 64
EPS = 1e-6

HEAD_PAD = 128
BF16_SUBLANES = 16
PEER_SUB = 512
SSM_CHUNK = 64
MXU_DTYPE = jnp.bfloat16
VMEM_LIMIT_BYTES = 56 * 1024 * 1024

_NT = (((1,), (1,)), ((), ()))
_TN = (((0,), (0,)), ((), ()))


def _dot(a, b):
    return jnp.dot(a, b, preferred_element_type=jnp.float32)


def _rms(x, g):
    return x * lax.rsqrt(jnp.mean(x * x, axis=-1, keepdims=True) + EPS) * g


def _params(*sem):
    return pltpu.CompilerParams(dimension_semantics=sem, vmem_limit_bytes=VMEM_LIMIT_BYTES)


def _full(shape):
    return pl.BlockSpec(shape, lambda *_: (0,) * len(shape))


def _inproj_kernel(x_ref, cos_ref, sin_ref, gmix_ref, wcat_ref, gq_ref, wq_ref, gkv_ref, wkv_ref,
                   q_ref, k_ref, v_ref, u_ref, gate_ref):
    x = x_ref[0]
    h = _rms(x, gmix_ref[...]).astype(MXU_DTYPE)
    cos = cos_ref[...]
    sin = sin_ref[...]
    o = 0
    hq = _dot(h, wcat_ref[:, o:o + Q_LORA_RANK]); o += Q_LORA_RANK
    hkv = _dot(h, wcat_ref[:, o:o + KV_LORA_RANK]); o += KV_LORA_RANK
    kr = _dot(h, wcat_ref[:, o:o + 2 * HEAD_PAD]); o += 2 * HEAD_PAD
    u_ref[...] = _dot(h, wcat_ref[:, o:o + SSM_WIDTH]).astype(u_ref.dtype); o += SSM_WIDTH
    gate_ref[...] = jax.nn.sigmoid(_dot(h, wcat_ref[:, o:])).astype(gate_ref.dtype)

    nq = MLA_HEADS * HEAD_PAD
    hqn = _rms(hq, gq_ref[...]).astype(MXU_DTYPE)
    qq = _dot(hqn, wq_ref[...])
    hkvn = _rms(hkv, gkv_ref[...]).astype(MXU_DTYPE)
    kv = _dot(hkvn, wkv_ref[...])
    k_rope = kr[:, :HEAD_PAD] * cos + kr[:, HEAD_PAD:] * sin
    scale = QK_HEAD_DIM ** -0.5 * math.log2(math.e)
    lane = lax.broadcasted_iota(jnp.int32, (1, HEAD_PAD), 1)
    ones_pad = (lane >= V_HEAD_DIM).astype(jnp.float32)
    for hd in range(MLA_HEADS):
        sl = slice(hd * HEAD_PAD, (hd + 1) * HEAD_PAD)
        q = qq[:, sl] * cos + qq[:, nq + hd * HEAD_PAD: nq + (hd + 1) * HEAD_PAD] * sin
        q_ref[0, hd] = (q * scale).astype(q_ref.dtype)
        k_ref[0, hd] = (kv[:, sl] + k_rope).astype(k_ref.dtype)
        v_ref[0, hd] = (kv[:, nq + hd * HEAD_PAD: nq + (hd + 1) * HEAD_PAD] + ones_pad).astype(v_ref.dtype)


def _inproj(x, cos_t, sin_t, gmix, wcat, gq, wq, gkv, wkv, tm):
    B, S, D = x.shape
    H = MLA_HEADS
    ncat = wcat.shape[1]
    ngate = ncat - (Q_LORA_RANK + KV_LORA_RANK + 2 * HEAD_PAD + SSM_WIDTH)
    grid = (B, S // tm)
    hs = jax.ShapeDtypeStruct((B, H, S, HEAD_PAD), MXU_DTYPE)
    head_spec = pl.BlockSpec((1, H, tm, HEAD_PAD), lambda b, s: (b, 0, s, 0))
    return pl.pallas_call(
        _inproj_kernel,
        grid=grid,
        in_specs=[
            pl.BlockSpec((1, tm, D), lambda b, s: (b, s, 0)),
            pl.BlockSpec((tm, HEAD_PAD), lambda b, s: (s, 0)),
            pl.BlockSpec((tm, HEAD_PAD), lambda b, s: (s, 0)),
            _full(gmix.shape), _full(wcat.shape), _full(gq.shape), _full(wq.shape),
            _full(gkv.shape), _full(wkv.shape),
        ],
        out_specs=[
            head_spec, head_spec, head_spec,
            pl.BlockSpec((tm, SSM_WIDTH), lambda b, s, n=S // tm: (b * n + s, 0)),
            pl.BlockSpec((tm, ngate), lambda b, s, n=S // tm: (b * n + s, 0)),
        ],
        out_shape=[hs, hs, hs,
                   jax.ShapeDtypeStruct((B * S, SSM_WIDTH), MXU_DTYPE),
                   jax.ShapeDtypeStruct((B * S, ngate), MXU_DTYPE)],
        compiler_params=_params("parallel", "parallel"),
        name="inproj",
    )(x, cos_t, sin_t, gmix, wcat, gq, wq, gkv, wkv)


def _attn_kernel(q_ref, k_ref, v_ref, o_ref, m_ref, acc_ref):
    ki = pl.program_id(3)

    @pl.when(ki == 0)
    def _():
        m_ref[...] = jnp.full(m_ref.shape, -jnp.inf, jnp.float32)
        acc_ref[...] = jnp.zeros(acc_ref.shape, jnp.float32)

    tk = k_ref.shape[2]
    for j in range(2):
        s = lax.dot_general(q_ref[0, j], k_ref[0, j], _NT, preferred_element_type=jnp.float32)
        m_prev = m_ref[j]
        m_new = jnp.maximum(m_prev, jnp.max(s, axis=1, keepdims=True))
        p = jnp.exp2((s - jnp.tile(m_new, (1, tk // HEAD_PAD))).astype(MXU_DTYPE))
        alpha = jnp.exp2(m_prev - m_new)
        acc_ref[j] = alpha * acc_ref[j] + _dot(p, v_ref[0, j])
        m_ref[j] = m_new

    @pl.when(ki == pl.num_programs(3) - 1)
    def _():
        a0, a1 = acc_ref[0], acc_ref[1]
        lane = lax.broadcasted_iota(jnp.int32, a0.shape, 1)
        o0 = a0 / pltpu.roll(a0, V_HEAD_DIM, 1)
        o1 = pltpu.roll(a1, V_HEAD_DIM, 1) / a1
        o_ref[0] = jnp.where(lane < V_HEAD_DIM, o0, o1).astype(o_ref.dtype)


def _attention(q, k, v, tq, tk):
    B, H, S, _ = q.shape
    grid = (B, H // 2, S // tq, S // tk)
    return pl.pallas_call(
        _attn_kernel,
        grid=grid,
        in_specs=[
            pl.BlockSpec((1, 2, tq, HEAD_PAD), lambda b, h, i, j: (b, h, i, 0)),
            pl.BlockSpec((1, 2, tk, HEAD_PAD), lambda b, h, i, j: (b, h, j, 0)),
            pl.BlockSpec((1, 2, tk, HEAD_PAD), lambda b, h, i, j: (b, h, j, 0)),
        ],
        out_specs=pl.BlockSpec((1, tq, 2 * V_HEAD_DIM), lambda b, h, i, j: (b, i, h)),
        out_shape=jax.ShapeDtypeStruct((B, S, H * V_HEAD_DIM), MXU_DTYPE),
        scratch_shapes=[pltpu.VMEM((2, tq, HEAD_PAD), jnp.float32),
                        pltpu.VMEM((2, tq, HEAD_PAD), jnp.float32)],
        compiler_params=_params("parallel", "parallel", "parallel", "arbitrary"),
        name="attention",
    )(q, k, v)


def _s5_kernel(u_ref, mt_ref, win_ref, wout_ref, dec_ref, y_ref, *, chunks_per_seq):
    u = u_ref[0]
    y = _dot(u, mt_ref[0])
    loc = _dot(u, win_ref[0])
    nc = u.shape[0]
    half = SSM_STATE
    cidx = lax.broadcasted_iota(jnp.int32, (nc, 2 * half), 0) % chunks_per_seq
    xf = loc[:, :2 * half]
    xb = loc[:, 2 * half:]
    levels = chunks_per_seq.bit_length() - 1
    for lv in range(levels):
        sh = 1 << lv
        prev = jnp.where(cidx >= sh, pltpu.roll(xf, sh, 0), 0.0)
        xf = xf + prev * dec_ref[0, 4 * lv + 0:4 * lv + 1, :] \
            + pltpu.roll(prev, half, 1) * dec_ref[0, 4 * lv + 1:4 * lv + 2, :]
        nxt = jnp.where(cidx < chunks_per_seq - sh, pltpu.roll(xb, nc - sh, 0), 0.0)
        xb = xb + nxt * dec_ref[0, 4 * lv + 2:4 * lv + 3, :] \
            + pltpu.roll(nxt, half, 1) * dec_ref[0, 4 * lv + 3:4 * lv + 4, :]
    xin_f = jnp.where(cidx >= 1, pltpu.roll(xf, 1, 0), 0.0)
    xin_b = jnp.where(cidx < chunks_per_seq - 1, pltpu.roll(xb, nc - 1, 0), 0.0)
    xin = jnp.concatenate([xin_f, xin_b], axis=1).astype(MXU_DTYPE)
    y = y + _dot(xin, wout_ref[0])
    y_ref[0] = jax.nn.gelu(y).astype(y_ref.dtype)


def _s5(u_g, mt, win, wout, dec, chunks_per_seq):
    G, NC, W = u_g.shape
    kern = functools.partial(_s5_kernel, chunks_per_seq=chunks_per_seq)
    blk = lambda a: pl.BlockSpec((1,) + a.shape[1:], lambda g: (g, 0, 0))
    return pl.pallas_call(
        kern,
        grid=(G,),
        in_specs=[blk(u_g), blk(mt), blk(win), blk(wout), blk(dec)],
        out_specs=pl.BlockSpec((1, NC, W), lambda g: (g, 0, 0)),
        out_shape=jax.ShapeDtypeStruct((G, NC, W), MXU_DTYPE),
        compiler_params=_params("parallel"),
        name="s5_scan",
    )(u_g, mt, win, wout, dec)


def _cmul(ar, ai, br, bi):
    return ar * br - ai * bi, ar * bi + ai * br


def _s5_operators(lam_re, lam_im, log_dt, b_re, b_im, c_re, c_im, d_skip, chunks_per_seq):
    L, G, P, Hg = SSM_CHUNK, SSM_GROUPS, SSM_STATE, SSM_GROUP
    f32 = jnp.float32
    hi = lax.Precision.HIGHEST
    lr, li = lam_re.astype(f32), lam_im.astype(f32)
    dt = jnp.exp(log_dt.astype(f32))[..., None]
    ar, ai = lr * dt, li * dt

    def powers(n):
        mag = jnp.exp(ar[:, :, None, :] * n[None, None, :, None])
        ang = ai[:, :, None, :] * n[None, None, :, None]
        return mag * jnp.cos(ang), mag * jnp.sin(ang)

    pr, pi = powers(jnp.arange(L + 1, dtype=f32))
    nr, ni = pr[:, :, 1] - 1.0, pi[:, :, 1]
    den = lr * lr + li * li
    qr, qi = (nr * lr + ni * li) / den, (ni * lr - nr * li) / den
    bbr, bbi = _cmul(qr[..., None], qi[..., None], b_re.astype(f32), b_im.astype(f32))
    cr, ci = c_re.astype(f32), c_im.astype(f32)
    wr, wi = _cmul(cr[:, :, None], ci[:, :, None], pr[:, :, :L, None, :], pi[:, :, :L, None, :])
    kern = (jnp.einsum('xgdip,xgpj->xgdij', wr, bbr, precision=hi)
            - jnp.einsum('xgdip,xgpj->xgdij', wi, bbi, precision=hi))
    k0 = kern[0][:, :1] + kern[1][:, :1] + jnp.eye(Hg, dtype=f32)[None, None] * d_skip.astype(f32).reshape(G, 1, Hg, 1)
    period = jnp.concatenate([k0, kern[0][:, 1:], jnp.zeros((G, 2, Hg, Hg), f32), kern[1][:, :0:-1]], axis=1)
    period = period.transpose(0, 3, 1, 2).reshape(G, Hg, (2 * L + 1) * Hg)
    rows = jnp.tile(period, (1, 1, L))[:, :, :L * 2 * L * Hg].reshape(G, Hg, L, 2 * L * Hg)[..., :L * Hg]
    mt = rows.astype(MXU_DTYPE).transpose(0, 2, 1, 3).reshape(G, L * Hg, L * Hg)
    to_in = lambda r, i: jnp.concatenate([r, i], axis=2).transpose(0, 1, 3, 2).reshape(G, L * Hg, 2 * P)
    wf = _cmul(pr[0][:, ::-1][:, 1:, :, None], pi[0][:, ::-1][:, 1:, :, None], bbr[0][:, None], bbi[0][:, None])
    wb = _cmul(pr[1][:, :L, :, None], pi[1][:, :L, :, None], bbr[1][:, None], bbi[1][:, None])
    win = jnp.concatenate([to_in(*wf), to_in(*wb)], axis=2)
    to_out = lambda r, i: jnp.concatenate([r, -i], axis=3).transpose(0, 3, 1, 2).reshape(G, 2 * P, L * Hg)
    of = _cmul(cr[0][:, None], ci[0][:, None], pr[0][:, 1:, None, :], pi[0][:, 1:, None, :])
    ob = _cmul(cr[1][:, None], ci[1][:, None], pr[1][:, ::-1][:, :L, None, :], pi[1][:, ::-1][:, :L, None, :])
    wout = jnp.concatenate([to_out(*of), to_out(*ob)], axis=1)
    levels = max(chunks_per_seq.bit_length() - 1, 1)
    dr, di = powers(L * (2.0 ** jnp.arange(levels, dtype=f32)))
    rows = jnp.stack([jnp.concatenate([dr, dr], -1), jnp.concatenate([-di, di], -1)], axis=3)
    dec = rows.transpose(1, 2, 0, 3, 4).reshape(G, levels * 4, 2 * P)
    return mt, win.astype(MXU_DTYPE), wout.astype(MXU_DTYPE), dec.astype(f32)


def _mix_kernel(x_ref, o_ref, gy_ref, gate_ref, woa_ref, wglu_ref, wos_ref, wout_ref, gffn_ref,
                x1_ref, h2t_ref):
    D = x_ref.shape[1]
    ya = _dot(o_ref[...], woa_ref[...])
    z = _dot(gy_ref[...], wglu_ref[...])
    yg = z[:, :SSM_WIDTH] * jax.nn.sigmoid(z[:, SSM_WIDTH:])
    ys = _dot(yg.astype(MXU_DTYPE), wos_ref[...])
    g = gate_ref[...].astype(jnp.float32)
    mixed = g[:, :D] * ya + g[:, D:] * ys
    x1 = x_ref[...] + _dot(mixed.astype(MXU_DTYPE), wout_ref[...])
    x1_ref[...] = x1
    h2t_ref[...] = _rms(x1, gffn_ref[...]).T.astype(h2t_ref.dtype)


def _mix(x2, o2, gy, gates, woa, wglu, wos, wout, gffn, tm):
    T, D = x2.shape
    row = lambda a: pl.BlockSpec((tm, a.shape[1]), lambda i: (i, 0))
    return pl.pallas_call(
        _mix_kernel,
        grid=(T // tm,),
        in_specs=[row(x2), row(o2), row(gy), row(gates),
                  _full(woa.shape), _full(wglu.shape), _full(wos.shape), _full(wout.shape), _full(gffn.shape)],
        out_specs=[pl.BlockSpec((tm, D), lambda i: (i, 0)), pl.BlockSpec((D, tm), lambda i: (0, i))],
        out_shape=[jax.ShapeDtypeStruct((T, D), jnp.float32), jax.ShapeDtypeStruct((D, T), MXU_DTYPE)],
        compiler_params=_params("parallel"),
        name="mix",
    )(x2, o2, gy, gates, woa, wglu, wos, wout, gffn)


def _extract_top(s, k, out_ref=None):
    kth = None
    rank = jnp.full(s.shape, float(k), jnp.float32)
    for i in range(k):
        kth = jnp.max(s, axis=0, keepdims=True)
        if out_ref is not None:
            out_ref[i:i + 1, :] = kth
        hit = s == kth
        rank = jnp.where(hit, float(i), rank)
        s = jnp.where(hit, -jnp.inf, s)
    return kth, rank


def _peer_stats_kernel(h2t_ref, wqt_ref, keys_ref, n1_ref, c_ref, r2_ref, e2_ref,
                       top1_ref, top2_ref, cand_ref):
    K = PEER_TOPK
    qpt = _dot(wqt_ref[...], h2t_ref[...])
    for hd in range(PEER_HEADS):
        r0 = hd * 2 * PEER_HALF
        s1 = _dot(keys_ref[2 * hd], qpt[r0:r0 + PEER_HALF].astype(MXU_DTYPE))
        s2 = _dot(keys_ref[2 * hd + 1], qpt[r0 + PEER_HALF:r0 + 2 * PEER_HALF].astype(MXU_DTYPE))
        _extract_top(s1, K, top1_ref)
        _, rank2 = _extract_top(s2, K, top2_ref)
        top2 = top2_ref[...]
        for i in range(K):
            cand_ref[i * K:(i + 1) * K, :] = top1_ref[i:i + 1, :] + top2
        cand = cand_ref[...]
        tau, _ = _extract_top(cand, K)
        m1 = top1_ref[0:1, :]
        m2 = top2_ref[0:1, :]
        z = jnp.sum(jnp.where(cand >= tau, jnp.exp(cand - (m1 + m2)), 0.0), axis=0, keepdims=True)
        n1 = jnp.zeros(s1.shape, jnp.float32)
        for r in range(K):
            n1 = n1 + jnp.where(s1 + top2_ref[r:r + 1, :] >= tau, 1.0, 0.0)
        n1_ref[hd] = n1
        c_ref[hd] = 0.5 * jnp.exp(s1 - m1) / z
        r2_ref[hd] = rank2.astype(r2_ref.dtype)
        e2_ref[hd] = jnp.exp(s2 - m2).astype(e2_ref.dtype)


def _peer_stats(h2t, wqt, keys, tm):
    D, T = h2t.shape
    st_spec = pl.BlockSpec((PEER_HEADS, N_KEYS, tm), lambda i: (0, 0, i))
    st = lambda dt: jax.ShapeDtypeStruct((PEER_HEADS, N_KEYS, T), dt)
    return pl.pallas_call(
        _peer_stats_kernel,
        grid=(T // tm,),
        in_specs=[pl.BlockSpec((D, tm), lambda i: (0, i)), _full(wqt.shape), _full(keys.shape)],
        out_specs=[st_spec] * 4,
        out_shape=[st(jnp.float32), st(jnp.float32), st(MXU_DTYPE), st(MXU_DTYPE)],
        scratch_shapes=[pltpu.VMEM((PEER_TOPK, tm), jnp.float32),
                        pltpu.VMEM((PEER_TOPK, tm), jnp.float32),
                        pltpu.VMEM((PEER_TOPK * PEER_TOPK, tm), jnp.float32)],
        compiler_params=_params("parallel"),
        name="peer_stats",
    )(h2t, wqt, keys)


def _gelu_doubled(x):
    c0 = math.sqrt(2.0 / math.pi)
    inner = x * (c0 + (c0 * 0.044715) * (x * x))
    return x + x * jnp.tanh(inner)


def _peer_kernel(h2t_ref, wd_ref, wu_ref, n1_ref, c_ref, r2_ref, e2_ref, x1_ref, gfin_ref,
                 out_ref, acc_ref, act_ref):
    e = pl.program_id(1)

    @pl.when(e == 0)
    def _():
        acc_ref[...] = jnp.zeros(acc_ref.shape, jnp.float32)

    te, tm = wd_ref.shape[0], h2t_ref.shape[1]
    pk = BF16_SUBLANES
    nb = N_KEYS // pk
    blocks_per_sub = PEER_SUB // N_KEYS

    total = None
    for k in range(te // PEER_SUB):
        rows = slice(k * PEER_SUB, (k + 1) * PEER_SUB)
        a_t = _dot(wd_ref[rows, :], h2t_ref[...])
        for j in range(blocks_per_sub):
            i1 = (e * (te // PEER_SUB) + k) * blocks_per_sub + j
            gate = None
            for hd in range(PEER_HEADS):
                n1 = jnp.broadcast_to(n1_ref[hd, pl.ds(i1, 1), :], (pk, tm)).astype(MXU_DTYPE)
                cw = jnp.broadcast_to(c_ref[hd, pl.ds(i1, 1), :], (pk, tm)).astype(MXU_DTYPE)
                r2 = r2_ref[hd].reshape(nb, pk, tm)
                e2 = e2_ref[hd].reshape(nb, pk, tm)
                g = jnp.where(r2 < n1[None], e2, jnp.zeros_like(e2)) * cw[None]
                gate = g if gate is None else gate + g
            blk = a_t[j * N_KEYS:(j + 1) * N_KEYS].astype(MXU_DTYPE).reshape(nb, pk, tm)
            act = _gelu_doubled(blk) * gate
            act_ref[k * PEER_SUB + j * N_KEYS:k * PEER_SUB + (j + 1) * N_KEYS, :] = act.reshape(N_KEYS, tm)
        part = lax.dot_general(act_ref[rows, :], wu_ref[rows, :], _TN, preferred_element_type=jnp.float32)
        total = part if total is None else total + part
    acc_ref[...] += total

    @pl.when(e == pl.num_programs(1) - 1)
    def _():
        out_ref[...] = _rms(x1_ref[...] + acc_ref[...], gfin_ref[...])


def _peer(h2t, wd, wu, n1, cw, r2, e2, x1, gfin, tm, te):
    D, T = h2t.shape
    E = wd.shape[0]
    st_spec = pl.BlockSpec((PEER_HEADS, N_KEYS, tm), lambda i, e: (0, 0, i))
    return pl.pallas_call(
        _peer_kernel,
        grid=(T // tm, E // te),
        in_specs=[pl.BlockSpec((D, tm), lambda i, e: (0, i)),
                  pl.BlockSpec((te, D), lambda i, e: (e, 0)),
                  pl.BlockSpec((te, D), lambda i, e: (e, 0)),
                  st_spec, st_spec, st_spec, st_spec,
                  pl.BlockSpec((tm, D), lambda i, e: (i, 0)),
                  pl.BlockSpec(gfin.shape, lambda i, e: (0, 0))],
        out_specs=pl.BlockSpec((tm, D), lambda i, e: (i, 0)),
        out_shape=jax.ShapeDtypeStruct((T, D), jnp.float32),
        scratch_shapes=[pltpu.VMEM((tm, D), jnp.float32), pltpu.VMEM((te, tm), MXU_DTYPE)],
        compiler_params=_params("parallel", "arbitrary"),
        name="peer_experts",
    )(h2t, wd, wu, n1, cw, r2, e2, x1, gfin)


def _head_cols(w, per_head, start, width, dst):
    K = w.shape[0]
    blk = w.reshape(K, MLA_HEADS, per_head)[:, :, start:start + width]
    blk = jnp.pad(blk, ((0, 0), (0, 0), (dst, HEAD_PAD - dst - width)))
    return blk.reshape(K, MLA_HEADS * HEAD_PAD)


def _tile(n, pref):
    return pref if n % pref == 0 else n


def kernel(x, norm_mix, w_in, q_a_norm, w_q_b, kv_a_norm, w_kv_b, w_o_attn, lam_re, lam_im, log_dt, b_re, b_im, c_re, c_im, d_skip, w_glu, w_o_ssm, w_out, norm_ffn, w_query, sub_keys, w_down, w_up, final_norm):
    B, S, D = x.shape
    T = B * S
    f32 = jnp.float32
    cd = MXU_DTYPE
    half = QK_ROPE_DIM // 2
    l = 0

    pos = jnp.arange(S, dtype=f32)
    inv_freq = 1.0 / (ROPE_THETA ** (jnp.arange(0, QK_ROPE_DIM, 2, dtype=f32) / QK_ROPE_DIM))
    ang = pos[:, None] * inv_freq[None, :]
    cos, sin = jnp.cos(ang), jnp.sin(ang)
    pad = HEAD_PAD - QK_HEAD_DIM
    cos_t = jnp.concatenate([jnp.ones((S, QK_NOPE_DIM), f32), cos, cos, jnp.zeros((S, pad), f32)], axis=1)
    sin_t = jnp.concatenate([jnp.zeros((S, QK_NOPE_DIM), f32), -sin, sin, jnp.zeros((S, pad), f32)], axis=1)

    w = w_in[l]
    c0, c1, c2, c3 = Q_LORA_RANK, Q_LORA_RANK + KV_LORA_RANK, Q_LORA_RANK + KV_LORA_RANK + QK_ROPE_DIM, \
        Q_LORA_RANK + KV_LORA_RANK + QK_ROPE_DIM + SSM_WIDTH
    w_kr = w[:, c1:c2]
    zl = jnp.zeros((D, QK_NOPE_DIM), f32)
    zr = jnp.zeros((D, pad), f32)
    kr_plain = jnp.concatenate([zl, w_kr, zr], axis=1)
    kr_swap = jnp.concatenate([zl, w_kr[:, half:], w_kr[:, :half], zr], axis=1)
    wcat = jnp.concatenate([w[:, :c0], w[:, c0:c1], kr_plain, kr_swap, w[:, c2:c3], w[:, c3:]], axis=1).astype(cd)

    wq = w_q_b[l]
    wq_plain = _head_cols(wq, QK_HEAD_DIM, 0, QK_HEAD_DIM, 0)
    wq_swap = (_head_cols(wq, QK_HEAD_DIM, QK_NOPE_DIM + half, half, QK_NOPE_DIM)
               + _head_cols(wq, QK_HEAD_DIM, QK_NOPE_DIM, half, QK_NOPE_DIM + half))
    wq_cat = jnp.concatenate([wq_plain, wq_swap], axis=1).astype(cd)
    wkv = w_kv_b[l]
    wkv_cat = jnp.concatenate([_head_cols(wkv, QK_NOPE_DIM + V_HEAD_DIM, 0, QK_NOPE_DIM, 0),
                               _head_cols(wkv, QK_NOPE_DIM + V_HEAD_DIM, QK_NOPE_DIM, V_HEAD_DIM, 0)],
                              axis=1).astype(cd)

    tm = _tile(S, 256)
    q, k, v, u, gates = _inproj(x, cos_t, sin_t, norm_mix[l][None], wcat, q_a_norm[l][None], wq_cat,
                                kv_a_norm[l][None], wkv_cat, tm)

    o = _attention(q, k, v, _tile(S, 512), _tile(S, 512))

    L, G, Hg = SSM_CHUNK, SSM_GROUPS, SSM_GROUP
    ncs = S // L
    mt, win, wout_s, dec = _s5_operators(lam_re[l], lam_im[l], log_dt[l], b_re[l], b_im[l], c_re[l], c_im[l],
                                         d_skip[l], ncs)
    u_g = u.reshape(B, ncs, L, G, Hg).transpose(3, 0, 1, 2, 4).reshape(G, B * ncs, L * Hg)
    gy_g = _s5(u_g, mt, win, wout_s, dec, ncs)
    gy = gy_g.reshape(G, B, ncs, L, Hg).transpose(1, 2, 3, 0, 4).reshape(T, SSM_WIDTH)

    x1, h2t = _mix(x.reshape(T, D), o.reshape(T, MLA_HEADS * V_HEAD_DIM), gy, gates,
                  w_o_attn[l].astype(cd), w_glu[l].astype(cd), w_o_ssm[l].astype(cd), w_out[l].astype(cd),
                  norm_ffn[l][None], _tile(T, 512))

    keys = sub_keys[l].reshape(PEER_HEADS * 2, N_KEYS, PEER_HALF).astype(cd)
    n1, cw, r2, e2 = _peer_stats(h2t, w_query[l].T.astype(cd), keys, _tile(T, 256))
    out = _peer(h2t, w_down[l].astype(cd), w_up[l].astype(cd), n1, cw, r2, e2, x1, final_norm[None],
                _tile(T, 512), 2048)
    return out.reshape(B, S, D)
```

```python
import functools
import math

import jax
import jax.numpy as jnp
from jax import lax
from jax.experimental import pallas as pl
from jax.experimental.pallas import tpu as pltpu

MLA_HEADS = 8
QK_NOPE_DIM = 64
QK_ROPE_DIM = 32
QK_HEAD_DIM = QK_NOPE_DIM + QK_ROPE_DIM
V_HEAD_DIM = 64
Q_LORA_RANK = 256
KV_LORA_RANK = 128
ROPE_THETA = 10000.0
SSM_WIDTH = 512
SSM_GROUP = 16
SSM_GROUPS = SSM_WIDTH // SSM_GROUP
SSM_STATE = 64
PEER_HEADS = 8
N_KEYS = 128
PEER_TOPK = 16
PEER_HALF = 64
EPS = 1e-6

HEAD_PAD = 128
BF16_SUBLANES = 16
PEER_SUB = 512
PEER_CAND_BLOCKS = ((0, 1, 16), (1, 4, 8), (4, 8, 4))
PEER_CAND_ROWS = sum((hi - lo) * keep for lo, hi, keep in PEER_CAND_BLOCKS) + PEER_TOPK // 2
ATTN_HEADS_PER_STEP = 4
ATTN_KEY_CHUNK = 512
SSM_CHUNK = 64
MXU_DTYPE = jnp.bfloat16
VMEM_LIMIT_BYTES = 56 * 1024 * 1024

_NT = (((1,), (1,)), ((), ()))
_TN = (((0,), (0,)), ((), ()))


def _dot(a, b):
    return jnp.dot(a, b, preferred_element_type=jnp.float32)


def _rms(x, g):
    return x * lax.rsqrt(jnp.mean(x * x, axis=-1, keepdims=True) + EPS) * g


def _params(*sem):
    return pltpu.CompilerParams(dimension_semantics=sem, vmem_limit_bytes=VMEM_LIMIT_BYTES)


def _full(shape):
    return pl.BlockSpec(shape, lambda *_: (0,) * len(shape))


def _inproj_kernel(x_ref, cos_ref, sin_ref, gmix_ref, wcat_ref, gq_ref, wq_ref, gkv_ref, wkv_ref,
                   q_ref, k_ref, v_ref, u_ref, gate_ref):
    x = x_ref[0]
    h = _rms(x, gmix_ref[...]).astype(MXU_DTYPE)
    cos = cos_ref[...]
    sin = sin_ref[...]
    o = 0
    hq = _dot(h, wcat_ref[:, o:o + Q_LORA_RANK]); o += Q_LORA_RANK
    hkv = _dot(h, wcat_ref[:, o:o + KV_LORA_RANK]); o += KV_LORA_RANK
    kr = _dot(h, wcat_ref[:, o:o + 2 * HEAD_PAD]); o += 2 * HEAD_PAD
    u_ref[...] = _dot(h, wcat_ref[:, o:o + SSM_WIDTH]).astype(u_ref.dtype); o += SSM_WIDTH
    gate_ref[...] = jax.nn.sigmoid(_dot(h, wcat_ref[:, o:])).astype(gate_ref.dtype)

    nq = MLA_HEADS * HEAD_PAD
    hqn = _rms(hq, gq_ref[...]).astype(MXU_DTYPE)
    qq = _dot(hqn, wq_ref[...])
    hkvn = _rms(hkv, gkv_ref[...]).astype(MXU_DTYPE)
    kv = _dot(hkvn, wkv_ref[...])
    k_rope = kr[:, :HEAD_PAD] * cos + kr[:, HEAD_PAD:] * sin
    scale = QK_HEAD_DIM ** -0.5 * math.log2(math.e)
    lane = lax.broadcasted_iota(jnp.int32, (1, HEAD_PAD), 1)
    ones_pad = (lane >= V_HEAD_DIM).astype(jnp.float32)
    for hd in range(MLA_HEADS):
        sl = slice(hd * HEAD_PAD, (hd + 1) * HEAD_PAD)
        q = qq[:, sl] * cos + qq[:, nq + hd * HEAD_PAD: nq + (hd + 1) * HEAD_PAD] * sin
        q_ref[0, hd] = (q * scale).astype(q_ref.dtype)
        k_ref[0, hd] = (kv[:, sl] + k_rope).astype(k_ref.dtype)
        v_ref[0, hd] = (kv[:, nq + hd * HEAD_PAD: nq + (hd + 1) * HEAD_PAD] + ones_pad).astype(v_ref.dtype)


def _inproj(x, cos_t, sin_t, gmix, wcat, gq, wq, gkv, wkv, tm):
    B, S, D = x.shape
    H = MLA_HEADS
    ncat = wcat.shape[1]
    ngate = ncat - (Q_LORA_RANK + KV_LORA_RANK + 2 * HEAD_PAD + SSM_WIDTH)
    grid = (B, S // tm)
    hs = jax.ShapeDtypeStruct((B, H, S, HEAD_PAD), MXU_DTYPE)
    head_spec = pl.BlockSpec((1, H, tm, HEAD_PAD), lambda b, s: (b, 0, s, 0))
    return pl.pallas_call(
        _inproj_kernel,
        grid=grid,
        in_specs=[
            pl.BlockSpec((1, tm, D), lambda b, s: (b, s, 0)),
            pl.BlockSpec((tm, HEAD_PAD), lambda b, s: (s, 0)),
            pl.BlockSpec((tm, HEAD_PAD), lambda b, s: (s, 0)),
            _full(gmix.shape), _full(wcat.shape), _full(gq.shape), _full(wq.shape),
            _full(gkv.shape), _full(wkv.shape),
        ],
        out_specs=[
            head_spec, head_spec, head_spec,
            pl.BlockSpec((tm, SSM_WIDTH), lambda b, s, n=S // tm: (b * n + s, 0)),
            pl.BlockSpec((tm, ngate), lambda b, s, n=S // tm: (b * n + s, 0)),
        ],
        out_shape=[hs, hs, hs,
                   jax.ShapeDtypeStruct((B * S, SSM_WIDTH), MXU_DTYPE),
                   jax.ShapeDtypeStruct((B * S, ngate), MXU_DTYPE)],
        compiler_params=_params("parallel", "parallel"),
        name="inproj",
    )(x, cos_t, sin_t, gmix, wcat, gq, wq, gkv, wkv)


def _attn_kernel(q_ref, k_ref, v_ref, o_ref, m_ref, acc_ref):
    ki = pl.program_id(3)

    @pl.when(ki == 0)
    def _():
        m_ref[...] = jnp.full(m_ref.shape, -jnp.inf, jnp.float32)
        acc_ref[...] = jnp.zeros(acc_ref.shape, jnp.float32)

    tk = k_ref.shape[2]
    chunk = min(ATTN_KEY_CHUNK, tk)
    for j in range(ATTN_HEADS_PER_STEP):
        q = q_ref[0, j]
        m = m_ref[j]
        acc = acc_ref[j]
        for c in range(tk // chunk):
            keys = slice(c * chunk, (c + 1) * chunk)
            s = lax.dot_general(q, k_ref[0, j, keys, :], _NT, preferred_element_type=jnp.float32)
            m_new = jnp.maximum(m, jnp.max(s, axis=1, keepdims=True))
            p = jnp.exp2((s - jnp.tile(m_new, (1, chunk // HEAD_PAD))).astype(MXU_DTYPE))
            acc = jnp.exp2(m - m_new) * acc + _dot(p, v_ref[0, j, keys, :])
            m = m_new
        m_ref[j] = m
        acc_ref[j] = acc

    @pl.when(ki == pl.num_programs(3) - 1)
    def _():
        pairs = []
        for j in range(0, ATTN_HEADS_PER_STEP, 2):
            a0, a1 = acc_ref[j], acc_ref[j + 1]
            lane = lax.broadcasted_iota(jnp.int32, a0.shape, 1)
            o0 = a0 / pltpu.roll(a0, V_HEAD_DIM, 1)
            o1 = pltpu.roll(a1, V_HEAD_DIM, 1) / a1
            pairs.append(jnp.where(lane < V_HEAD_DIM, o0, o1))
        o_ref[0] = jnp.concatenate(pairs, axis=1).astype(o_ref.dtype)


def _attention(q, k, v, tq, tk):
    B, H, S, _ = q.shape
    hp = ATTN_HEADS_PER_STEP
    grid = (B, H // hp, S // tq, S // tk)
    return pl.pallas_call(
        _attn_kernel,
        grid=grid,
        in_specs=[
            pl.BlockSpec((1, hp, tq, HEAD_PAD), lambda b, h, i, j: (b, h, i, 0)),
            pl.BlockSpec((1, hp, tk, HEAD_PAD), lambda b, h, i, j: (b, h, j, 0)),
            pl.BlockSpec((1, hp, tk, HEAD_PAD), lambda b, h, i, j: (b, h, j, 0)),
        ],
        out_specs=pl.BlockSpec((1, tq, hp * V_HEAD_DIM), lambda b, h, i, j: (b, i, h)),
        out_shape=jax.ShapeDtypeStruct((B, S, H * V_HEAD_DIM), MXU_DTYPE),
        scratch_shapes=[pltpu.VMEM((hp, tq, HEAD_PAD), jnp.float32),
                        pltpu.VMEM((hp, tq, HEAD_PAD), jnp.float32)],
        compiler_params=_params("parallel", "parallel", "parallel", "arbitrary"),
        name="attention",
    )(q, k, v)


def _s5_kernel(u_ref, strip_ref, win_ref, wout_ref, dec_ref, y_ref, mt_ref, *, chunks_per_seq):
    L, Hg = SSM_CHUNK, SSM_GROUP
    for s in range(L):
        off = (L - 1 - s) * Hg
        mt_ref[s * Hg:(s + 1) * Hg, :] = strip_ref[0, :, off:off + L * Hg].astype(mt_ref.dtype)
    u = u_ref[0]
    y = _dot(u, mt_ref[...])
    loc = _dot(u, win_ref[0])
    nc = u.shape[0]
    half = SSM_STATE
    cidx = lax.broadcasted_iota(jnp.int32, (nc, 2 * half), 0) % chunks_per_seq
    xf = loc[:, :2 * half]
    xb = loc[:, 2 * half:]
    levels = chunks_per_seq.bit_length() - 1
    for lv in range(levels):
        sh = 1 << lv
        prev = jnp.where(cidx >= sh, pltpu.roll(xf, sh, 0), 0.0)
        xf = xf + prev * dec_ref[0, 4 * lv + 0:4 * lv + 1, :] \
            + pltpu.roll(prev, half, 1) * dec_ref[0, 4 * lv + 1:4 * lv + 2, :]
        nxt = jnp.where(cidx < chunks_per_seq - sh, pltpu.roll(xb, nc - sh, 0), 0.0)
        xb = xb + nxt * dec_ref[0, 4 * lv + 2:4 * lv + 3, :] \
            + pltpu.roll(nxt, half, 1) * dec_ref[0, 4 * lv + 3:4 * lv + 4, :]
    xin_f = jnp.where(cidx >= 1, pltpu.roll(xf, 1, 0), 0.0)
    xin_b = jnp.where(cidx < chunks_per_seq - 1, pltpu.roll(xb, nc - 1, 0), 0.0)
    xin = jnp.concatenate([xin_f, xin_b], axis=1).astype(MXU_DTYPE)
    y = y + _dot(xin, wout_ref[0])
    y_ref[0] = jax.nn.gelu(y).astype(y_ref.dtype)


def _s5(u_g, strip, win, wout, dec, chunks_per_seq):
    G, NC, W = u_g.shape
    kern = functools.partial(_s5_kernel, chunks_per_seq=chunks_per_seq)
    blk = lambda a: pl.BlockSpec((1,) + a.shape[1:], lambda g: (g, 0, 0))
    return pl.pallas_call(
        kern,
        grid=(G,),
        in_specs=[blk(u_g), blk(strip), blk(win), blk(wout), blk(dec)],
        out_specs=pl.BlockSpec((1, NC, W), lambda g: (g, 0, 0)),
        out_shape=jax.ShapeDtypeStruct((G, NC, W), MXU_DTYPE),
        scratch_shapes=[pltpu.VMEM((W, W), MXU_DTYPE)],
        compiler_params=_params("parallel"),
        name="s5_scan",
    )(u_g, strip, win, wout, dec)


def _cmul(ar, ai, br, bi):
    return ar * br - ai * bi, ar * bi + ai * br


def _s5_operators(lam_re, lam_im, log_dt, b_re, b_im, c_re, c_im, d_skip, chunks_per_seq):
    L, G, P, Hg = SSM_CHUNK, SSM_GROUPS, SSM_STATE, SSM_GROUP
    f32 = jnp.float32
    hi = lax.Precision.HIGHEST
    lr, li = lam_re.astype(f32), lam_im.astype(f32)
    dt = jnp.exp(log_dt.astype(f32))[..., None]
    ar, ai = lr * dt, li * dt

    def powers(n):
        mag = jnp.exp(ar[:, :, None, :] * n[None, None, :, None])
        ang = ai[:, :, None, :] * n[None, None, :, None]
        return mag * jnp.cos(ang), mag * jnp.sin(ang)

    pr, pi = powers(jnp.arange(L + 1, dtype=f32))
    nr, ni = pr[:, :, 1] - 1.0, pi[:, :, 1]
    den = lr * lr + li * li
    qr, qi = (nr * lr + ni * li) / den, (ni * lr - nr * li) / den
    bbr, bbi = _cmul(qr[..., None], qi[..., None], b_re.astype(f32), b_im.astype(f32))
    cr, ci = c_re.astype(f32), c_im.astype(f32)
    wr, wi = _cmul(cr[:, :, None], ci[:, :, None], pr[:, :, :L, None, :], pi[:, :, :L, None, :])
    kern = (jnp.einsum('xgdip,xgpj->xgdij', wr, bbr, precision=hi)
            - jnp.einsum('xgdip,xgpj->xgdij', wi, bbi, precision=hi))
    k0 = kern[0][:, :1] + kern[1][:, :1] + jnp.eye(Hg, dtype=f32)[None, None] * d_skip.astype(f32).reshape(G, 1, Hg, 1)
    strip = jnp.concatenate([kern[1][:, :0:-1], k0, kern[0][:, 1:], jnp.zeros((G, 1, Hg, Hg), f32)], axis=1)
    strip = strip.transpose(0, 3, 1, 2).reshape(G, Hg, 2 * L * Hg)
    to_in = lambda r, i: jnp.concatenate([r, i], axis=2).transpose(0, 1, 3, 2).reshape(G, L * Hg, 2 * P)
    wf = _cmul(pr[0][:, ::-1][:, 1:, :, None], pi[0][:, ::-1][:, 1:, :, None], bbr[0][:, None], bbi[0][:, None])
    wb = _cmul(pr[1][:, :L, :, None], pi[1][:, :L, :, None], bbr[1][:, None], bbi[1][:, None])
    win = jnp.concatenate([to_in(*wf), to_in(*wb)], axis=2)
    to_out = lambda r, i: jnp.concatenate([r, -i], axis=3).transpose(0, 3, 1, 2).reshape(G, 2 * P, L * Hg)
    of = _cmul(cr[0][:, None], ci[0][:, None], pr[0][:, 1:, None, :], pi[0][:, 1:, None, :])
    ob = _cmul(cr[1][:, None], ci[1][:, None], pr[1][:, ::-1][:, :L, None, :], pi[1][:, ::-1][:, :L, None, :])
    wout = jnp.concatenate([to_out(*of), to_out(*ob)], axis=1)
    levels = max(chunks_per_seq.bit_length() - 1, 1)
    dr, di = powers(L * (2.0 ** jnp.arange(levels, dtype=f32)))
    rows = jnp.stack([jnp.concatenate([dr, dr], -1), jnp.concatenate([-di, di], -1)], axis=3)
    dec = rows.transpose(1, 2, 0, 3, 4).reshape(G, levels * 4, 2 * P)
    return strip, win.astype(MXU_DTYPE), wout.astype(MXU_DTYPE), dec.astype(f32)


def _mix_kernel(x_ref, o_ref, gy_ref, gate_ref, woa_ref, wglu_ref, wos_ref, wout_ref, gffn_ref,
                x1_ref, h2t_ref):
    D = x_ref.shape[1]
    ya = _dot(o_ref[...], woa_ref[...])
    z = _dot(gy_ref[...], wglu_ref[...])
    yg = z[:, :SSM_WIDTH] * jax.nn.sigmoid(z[:, SSM_WIDTH:])
    ys = _dot(yg.astype(MXU_DTYPE), wos_ref[...])
    g = gate_ref[...].astype(jnp.float32)
    mixed = g[:, :D] * ya + g[:, D:] * ys
    x1 = x_ref[...] + _dot(mixed.astype(MXU_DTYPE), wout_ref[...])
    x1_ref[...] = x1
    h2t_ref[...] = _rms(x1, gffn_ref[...]).T.astype(h2t_ref.dtype)


def _mix(x2, o2, gy, gates, woa, wglu, wos, wout, gffn, tm):
    T, D = x2.shape
    row = lambda a: pl.BlockSpec((tm, a.shape[1]), lambda i: (i, 0))
    return pl.pallas_call(
        _mix_kernel,
        grid=(T // tm,),
        in_specs=[row(x2), row(o2), row(gy), row(gates),
                  _full(woa.shape), _full(wglu.shape), _full(wos.shape), _full(wout.shape), _full(gffn.shape)],
        out_specs=[pl.BlockSpec((tm, D), lambda i: (i, 0)), pl.BlockSpec((D, tm), lambda i: (0, i))],
        out_shape=[jax.ShapeDtypeStruct((T, D), jnp.float32), jax.ShapeDtypeStruct((D, T), MXU_DTYPE)],
        compiler_params=_params("parallel"),
        name="mix",
    )(x2, o2, gy, gates, woa, wglu, wos, wout, gffn)


def _extract_top(s, k, out_ref=None):
    kth = None
    rank = jnp.full(s.shape, float(k), jnp.float32)
    for i in range(k):
        kth = jnp.max(s, axis=0, keepdims=True)
        if out_ref is not None:
            out_ref[i:i + 1, :] = kth
        hit = s == kth
        rank = jnp.where(hit, float(i), rank)
        s = jnp.where(hit, -jnp.inf, s)
    return kth, rank


def _peer_stats_kernel(h2t_ref, wqt_ref, keys_ref, n1_ref, c_ref, r2_ref, e2_ref,
                       top1_ref, top2_ref, cand_ref):
    K = PEER_TOPK
    qpt = _dot(wqt_ref[...], h2t_ref[...])
    for hd in range(PEER_HEADS):
        r0 = hd * 2 * PEER_HALF
        s1 = _dot(keys_ref[2 * hd], qpt[r0:r0 + PEER_HALF].astype(MXU_DTYPE))
        s2 = _dot(keys_ref[2 * hd + 1], qpt[r0 + PEER_HALF:r0 + 2 * PEER_HALF].astype(MXU_DTYPE))
        _extract_top(s1, K, top1_ref)
        _, rank2 = _extract_top(s2, K, top2_ref)
        row = 0
        for lo, hi, keep in PEER_CAND_BLOCKS:
            for r1 in range(lo, hi):
                cand_ref[row:row + keep, :] = top1_ref[r1:r1 + 1, :] + top2_ref[0:keep, :]
                row += keep
        cand_ref[row:row + K // 2, :] = top1_ref[K // 2:K, :] + top2_ref[0:1, :]
        cand = cand_ref[...]
        tau, _ = _extract_top(cand, K)
        m1 = top1_ref[0:1, :]
        m2 = top2_ref[0:1, :]
        z = jnp.sum(jnp.where(cand >= tau, jnp.exp(cand - (m1 + m2)), 0.0), axis=0, keepdims=True)
        n1 = jnp.zeros(s1.shape, jnp.float32)
        for r in range(K // 2):
            n1 = n1 + jnp.where(s1 + top2_ref[r:r + 1, :] >= tau, 1.0, 0.0)
        extra = jnp.zeros(m1.shape, jnp.float32)
        for r in range(K // 2, K):
            extra = extra + jnp.where(m1 + top2_ref[r:r + 1, :] >= tau, 1.0, 0.0)
        n1_ref[hd] = n1 + jnp.where(s1 == m1, extra, 0.0)
        c_ref[hd] = jnp.exp(s1 - m1) * (0.5 / z)
        r2_ref[hd] = rank2.astype(r2_ref.dtype)
        e2_ref[hd] = jnp.exp(s2 - m2).astype(e2_ref.dtype)


def _peer_stats(h2t, wqt, keys, tm):
    D, T = h2t.shape
    st_spec = pl.BlockSpec((PEER_HEADS, N_KEYS, tm), lambda i: (0, 0, i))
    st = lambda dt: jax.ShapeDtypeStruct((PEER_HEADS, N_KEYS, T), dt)
    return pl.pallas_call(
        _peer_stats_kernel,
        grid=(T // tm,),
        in_specs=[pl.BlockSpec((D, tm), lambda i: (0, i)), _full(wqt.shape), _full(keys.shape)],
        out_specs=[st_spec] * 4,
        out_shape=[st(jnp.float32), st(jnp.float32), st(MXU_DTYPE), st(MXU_DTYPE)],
        scratch_shapes=[pltpu.VMEM((PEER_TOPK, tm), jnp.float32),
                        pltpu.VMEM((PEER_TOPK, tm), jnp.float32),
                        pltpu.VMEM((PEER_CAND_ROWS, tm), jnp.float32)],
        compiler_params=_params("parallel"),
        name="peer_stats",
    )(h2t, wqt, keys)


def _gelu_doubled(x):
    c0 = math.sqrt(2.0 / math.pi)
    inner = x * (c0 + (c0 * 0.044715) * (x * x))
    return x + x * jnp.tanh(inner)


def _peer_kernel(h2t_ref, wd_ref, wu_ref, n1_ref, c_ref, r2_ref, e2_ref, x1_ref, gfin_ref,
                 out_ref, acc_ref, act_ref):
    e = pl.program_id(1)

    @pl.when(e == 0)
    def _():
        acc_ref[...] = jnp.zeros(acc_ref.shape, jnp.float32)

    te, tm = wd_ref.shape[0], h2t_ref.shape[1]
    pk = BF16_SUBLANES
    nb = N_KEYS // pk
    blocks_per_sub = PEER_SUB // N_KEYS

    total = None
    for k in range(te // PEER_SUB):
        rows = slice(k * PEER_SUB, (k + 1) * PEER_SUB)
        a_t = _dot(wd_ref[rows, :], h2t_ref[...])
        for j in range(blocks_per_sub):
            i1 = (e * (te // PEER_SUB) + k) * blocks_per_sub + j
            gate = None
            for hd in range(PEER_HEADS):
                n1 = jnp.broadcast_to(n1_ref[hd, pl.ds(i1, 1), :], (pk, tm)).astype(MXU_DTYPE)
                cw = jnp.broadcast_to(c_ref[hd, pl.ds(i1, 1), :], (pk, tm)).astype(MXU_DTYPE)
                r2 = r2_ref[hd].reshape(nb, pk, tm)
                e2 = e2_ref[hd].reshape(nb, pk, tm)
                g = jnp.where(r2 < n1[None], e2, jnp.zeros_like(e2)) * cw[None]
                gate = g if gate is None else gate + g
            blk = a_t[j * N_KEYS:(j + 1) * N_KEYS].astype(MXU_DTYPE).reshape(nb, pk, tm)
            act = _gelu_doubled(blk) * gate
            act_ref[k * PEER_SUB + j * N_KEYS:k * PEER_SUB + (j + 1) * N_KEYS, :] = act.reshape(N_KEYS, tm)
        part = lax.dot_general(act_ref[rows, :], wu_ref[rows, :], _TN, preferred_element_type=jnp.float32)
        total = part if total is None else total + part
    acc_ref[...] += total

    @pl.when(e == pl.num_programs(1) - 1)
    def _():
        out_ref[...] = _rms(x1_ref[...] + acc_ref[...], gfin_ref[...])


def _peer(h2t, wd, wu, n1, cw, r2, e2, x1, gfin, tm, te):
    D, T = h2t.shape
    E = wd.shape[0]
    st_spec = pl.BlockSpec((PEER_HEADS, N_KEYS, tm), lambda i, e: (0, 0, i))
    return pl.pallas_call(
        _peer_kernel,
        grid=(T // tm, E // te),
        in_specs=[pl.BlockSpec((D, tm), lambda i, e: (0, i)),
                  pl.BlockSpec((te, D), lambda i, e: (e, 0)),
                  pl.BlockSpec((te, D), lambda i, e: (e, 0)),
                  st_spec, st_spec, st_spec, st_spec,
                  pl.BlockSpec((tm, D), lambda i, e: (i, 0)),
                  pl.BlockSpec(gfin.shape, lambda i, e: (0, 0))],
        out_specs=pl.BlockSpec((tm, D), lambda i, e: (i, 0)),
        out_shape=jax.ShapeDtypeStruct((T, D), jnp.float32),
        scratch_shapes=[pltpu.VMEM((tm, D), jnp.float32), pltpu.VMEM((te, tm), MXU_DTYPE)],
        compiler_params=_params("parallel", "arbitrary"),
        name="peer_experts",
    )(h2t, wd, wu, n1, cw, r2, e2, x1, gfin)


def _head_cols(w, per_head, start, width, dst):
    K = w.shape[0]
    blk = w.reshape(K, MLA_HEADS, per_head)[:, :, start:start + width]
    blk = jnp.pad(blk, ((0, 0), (0, 0), (dst, HEAD_PAD - dst - width)))
    return blk.reshape(K, MLA_HEADS * HEAD_PAD)


def _tile(n, pref):
    return pref if n % pref == 0 else n


def kernel(x, norm_mix, w_in, q_a_norm, w_q_b, kv_a_norm, w_kv_b, w_o_attn, lam_re, lam_im, log_dt, b_re, b_im, c_re, c_im, d_skip, w_glu, w_o_ssm, w_out, norm_ffn, w_query, sub_keys, w_down, w_up, final_norm):
    B, S, D = x.shape
    T = B * S
    f32 = jnp.float32
    cd = MXU_DTYPE
    half = QK_ROPE_DIM // 2
    l = 0

    pos = jnp.arange(S, dtype=f32)
    inv_freq = 1.0 / (ROPE_THETA ** (jnp.arange(0, QK_ROPE_DIM, 2, dtype=f32) / QK_ROPE_DIM))
    ang = pos[:, None] * inv_freq[None, :]
    cos, sin = jnp.cos(ang), jnp.sin(ang)
    pad = HEAD_PAD - QK_HEAD_DIM
    cos_t = jnp.concatenate([jnp.ones((S, QK_NOPE_DIM), f32), cos, cos, jnp.zeros((S, pad), f32)], axis=1)
    sin_t = jnp.concatenate([jnp.zeros((S, QK_NOPE_DIM), f32), -sin, sin, jnp.zeros((S, pad), f32)], axis=1)

    w = w_in[l]
    c0, c1, c2, c3 = Q_LORA_RANK, Q_LORA_RANK + KV_LORA_RANK, Q_LORA_RANK + KV_LORA_RANK + QK_ROPE_DIM, \
        Q_LORA_RANK + KV_LORA_RANK + QK_ROPE_DIM + SSM_WIDTH
    w_kr = w[:, c1:c2]
    zl = jnp.zeros((D, QK_NOPE_DIM), f32)
    zr = jnp.zeros((D, pad), f32)
    kr_plain = jnp.concatenate([zl, w_kr, zr], axis=1)
    kr_swap = jnp.concatenate([zl, w_kr[:, half:], w_kr[:, :half], zr], axis=1)
    wcat = jnp.concatenate([w[:, :c0], w[:, c0:c1], kr_plain, kr_swap, w[:, c2:c3], w[:, c3:]], axis=1).astype(cd)

    wq = w_q_b[l]
    wq_plain = _head_cols(wq, QK_HEAD_DIM, 0, QK_HEAD_DIM, 0)
    wq_swap = (_head_cols(wq, QK_HEAD_DIM, QK_NOPE_DIM + half, half, QK_NOPE_DIM)
               + _head_cols(wq, QK_HEAD_DIM, QK_NOPE_DIM, half, QK_NOPE_DIM + half))
    wq_cat = jnp.concatenate([wq_plain, wq_swap], axis=1).astype(cd)
    wkv = w_kv_b[l]
    wkv_cat = jnp.concatenate([_head_cols(wkv, QK_NOPE_DIM + V_HEAD_DIM, 0, QK_NOPE_DIM, 0),
                               _head_cols(wkv, QK_NOPE_DIM + V_HEAD_DIM, QK_NOPE_DIM, V_HEAD_DIM, 0)],
                              axis=1).astype(cd)

    tm = _tile(S, 256)
    q, k, v, u, gates = _inproj(x, cos_t, sin_t, norm_mix[l][None], wcat, q_a_norm[l][None], wq_cat,
                                kv_a_norm[l][None], wkv_cat, tm)

    o = _attention(q, k, v, _tile(S, 512), _tile(S, 1024))

    L, G, Hg = SSM_CHUNK, SSM_GROUPS, SSM_GROUP
    ncs = S // L
    strip, win, wout_s, dec = _s5_operators(lam_re[l], lam_im[l], log_dt[l], b_re[l], b_im[l], c_re[l], c_im[l],
                                         d_skip[l], ncs)
    u_g = u.reshape(B, ncs, L, G, Hg).transpose(3, 0, 1, 2, 4).reshape(G, B * ncs, L * Hg)
    gy_g = _s5(u_g, strip, win, wout_s, dec, ncs)
    gy = gy_g.reshape(G, B, ncs, L, Hg).transpose(1, 2, 3, 0, 4).reshape(T, SSM_WIDTH)

    x1, h2t = _mix(x.reshape(T, D), o.reshape(T, MLA_HEADS * V_HEAD_DIM), gy, gates,
                  w_o_attn[l].astype(cd), w_glu[l].astype(cd), w_o_ssm[l].astype(cd), w_out[l].astype(cd),
                  norm_ffn[l][None], _tile(T, 512))

    keys = sub_keys[l].reshape(PEER_HEADS * 2, N_KEYS, PEER_HALF).astype(cd)
    n1, cw, r2, e2 = _peer_stats(h2t, w_query[l].T.astype(cd), keys, _tile(T, 256))
    out = _peer(h2t, w_down[l].astype(cd), w_up[l].astype(cd), n1, cw, r2, e2, x1, final_norm[None],
                _tile(T, 512), 2048)
    return out.reshape(B, S, D)
```

```python
import functools
import math

import jax
import jax.numpy as jnp
from jax import lax
from jax.experimental import pallas as pl
from jax.experimental.pallas import tpu as pltpu

MLA_HEADS = 8
QK_NOPE_DIM = 64
QK_ROPE_DIM = 32
QK_HEAD_DIM = QK_NOPE_DIM + QK_ROPE_DIM
V_HEAD_DIM = 64
Q_LORA_RANK = 256
KV_LORA_RANK = 128
ROPE_THETA = 10000.0
SSM_WIDTH = 512
SSM_GROUP = 16
SSM_GROUPS = SSM_WIDTH // SSM_GROUP
SSM_STATE = 64
PEER_HEADS = 8
N_KEYS = 128
PEER_TOPK = 16
PEER_HALF = 64
EPS = 1e-6

HEAD_PAD = 128
BF16_SUBLANES = 16
PEER_SUB = 512
PEER_CAND_BLOCKS = ((0, 1, 16), (1, 4, 8), (4, 8, 4))
PEER_CAND_ROWS = sum((hi - lo) * keep for lo, hi, keep in PEER_CAND_BLOCKS) + PEER_TOPK // 2
ATTN_HEADS_PER_STEP = 8
ATTN_KEY_CHUNK = 512
SSM_CHUNK = 64
MXU_DTYPE = jnp.bfloat16
VMEM_LIMIT_BYTES = 56 * 1024 * 1024

_NT = (((1,), (1,)), ((), ()))
_TN = (((0,), (0,)), ((), ()))


def _dot(a, b):
    return jnp.dot(a, b, preferred_element_type=jnp.float32)


def _rms(x, g):
    return x * lax.rsqrt(jnp.mean(x * x, axis=-1, keepdims=True) + EPS) * g


def _params(*sem):
    return pltpu.CompilerParams(dimension_semantics=sem, vmem_limit_bytes=VMEM_LIMIT_BYTES)


def _full(shape):
    return pl.BlockSpec(shape, lambda *_: (0,) * len(shape))


def _inproj_kernel(x_ref, cos_ref, sin_ref, gmix_ref, wcat_ref, gq_ref, wq_ref, gkv_ref, wkv_ref,
                   q_ref, k_ref, v_ref, u_ref, gate_ref):
    x = x_ref[0]
    h = _rms(x, gmix_ref[...]).astype(MXU_DTYPE)
    cos = cos_ref[...]
    sin = sin_ref[...]
    o = 0
    hq = _dot(h, wcat_ref[:, o:o + Q_LORA_RANK]); o += Q_LORA_RANK
    hkv = _dot(h, wcat_ref[:, o:o + KV_LORA_RANK]); o += KV_LORA_RANK
    kr = _dot(h, wcat_ref[:, o:o + 2 * HEAD_PAD]); o += 2 * HEAD_PAD
    u_ref[...] = _dot(h, wcat_ref[:, o:o + SSM_WIDTH]).astype(u_ref.dtype); o += SSM_WIDTH
    gate_ref[...] = jax.nn.sigmoid(_dot(h, wcat_ref[:, o:])).astype(gate_ref.dtype)

    nq = MLA_HEADS * HEAD_PAD
    hqn = _rms(hq, gq_ref[...]).astype(MXU_DTYPE)
    qq = _dot(hqn, wq_ref[...])
    hkvn = _rms(hkv, gkv_ref[...]).astype(MXU_DTYPE)
    kv = _dot(hkvn, wkv_ref[...])
    k_rope = kr[:, :HEAD_PAD] * cos + kr[:, HEAD_PAD:] * sin
    scale = QK_HEAD_DIM ** -0.5 * math.log2(math.e)
    lane = lax.broadcasted_iota(jnp.int32, (1, HEAD_PAD), 1)
    ones_pad = (lane >= V_HEAD_DIM).astype(jnp.float32)
    for hd in range(MLA_HEADS):
        sl = slice(hd * HEAD_PAD, (hd + 1) * HEAD_PAD)
        q = qq[:, sl] * cos + qq[:, nq + hd * HEAD_PAD: nq + (hd + 1) * HEAD_PAD] * sin
        q_ref[0, hd] = (q * scale).astype(q_ref.dtype)
        k_ref[0, hd] = (kv[:, sl] + k_rope).astype(k_ref.dtype)
        v_ref[0, hd] = (kv[:, nq + hd * HEAD_PAD: nq + (hd + 1) * HEAD_PAD] + ones_pad).astype(v_ref.dtype)


def _inproj(x, cos_t, sin_t, gmix, wcat, gq, wq, gkv, wkv, tm):
    B, S, D = x.shape
    H = MLA_HEADS
    ncat = wcat.shape[1]
    ngate = ncat - (Q_LORA_RANK + KV_LORA_RANK + 2 * HEAD_PAD + SSM_WIDTH)
    grid = (B, S // tm)
    hs = jax.ShapeDtypeStruct((B, H, S, HEAD_PAD), MXU_DTYPE)
    head_spec = pl.BlockSpec((1, H, tm, HEAD_PAD), lambda b, s: (b, 0, s, 0))
    return pl.pallas_call(
        _inproj_kernel,
        grid=grid,
        in_specs=[
            pl.BlockSpec((1, tm, D), lambda b, s: (b, s, 0)),
            pl.BlockSpec((tm, HEAD_PAD), lambda b, s: (s, 0)),
            pl.BlockSpec((tm, HEAD_PAD), lambda b, s: (s, 0)),
            _full(gmix.shape), _full(wcat.shape), _full(gq.shape), _full(wq.shape),
            _full(gkv.shape), _full(wkv.shape),
        ],
        out_specs=[
            head_spec, head_spec, head_spec,
            pl.BlockSpec((tm, SSM_WIDTH), lambda b, s, n=S // tm: (b * n + s, 0)),
            pl.BlockSpec((tm, ngate), lambda b, s, n=S // tm: (b * n + s, 0)),
        ],
        out_shape=[hs, hs, hs,
                   jax.ShapeDtypeStruct((B * S, SSM_WIDTH), MXU_DTYPE),
                   jax.ShapeDtypeStruct((B * S, ngate), MXU_DTYPE)],
        compiler_params=_params("parallel", "parallel"),
        name="inproj",
    )(x, cos_t, sin_t, gmix, wcat, gq, wq, gkv, wkv)


def _attn_kernel(q_ref, k_ref, v_ref, o_ref, m_ref, acc_ref):
    ki = pl.program_id(3)

    @pl.when(ki == 0)
    def _():
        m_ref[...] = jnp.full(m_ref.shape, -jnp.inf, jnp.float32)
        acc_ref[...] = jnp.zeros(acc_ref.shape, jnp.float32)

    tk = k_ref.shape[2]
    chunk = min(ATTN_KEY_CHUNK, tk)
    for j in range(ATTN_HEADS_PER_STEP):
        q = q_ref[0, j]
        m = m_ref[j]
        acc = acc_ref[j]
        for c in range(tk // chunk):
            keys = slice(c * chunk, (c + 1) * chunk)
            s = lax.dot_general(q, k_ref[0, j, keys, :], _NT, preferred_element_type=jnp.float32)
            m_new = jnp.maximum(m, jnp.max(s, axis=1, keepdims=True))
            p = jnp.exp2((s - jnp.tile(m_new, (1, chunk // HEAD_PAD))).astype(MXU_DTYPE))
            acc = jnp.exp2(m - m_new) * acc + _dot(p, v_ref[0, j, keys, :])
            m = m_new
        m_ref[j] = m
        acc_ref[j] = acc

    @pl.when(ki == pl.num_programs(3) - 1)
    def _():
        pairs = []
        for j in range(0, ATTN_HEADS_PER_STEP, 2):
            a0, a1 = acc_ref[j], acc_ref[j + 1]
            lane = lax.broadcasted_iota(jnp.int32, a0.shape, 1)
            o0 = a0 / pltpu.roll(a0, V_HEAD_DIM, 1)
            o1 = pltpu.roll(a1, V_HEAD_DIM, 1) / a1
            pairs.append(jnp.where(lane < V_HEAD_DIM, o0, o1))
        o_ref[0] = jnp.concatenate(pairs, axis=1).astype(o_ref.dtype)


def _attention(q, k, v, tq, tk):
    B, H, S, _ = q.shape
    hp = ATTN_HEADS_PER_STEP
    grid = (B, H // hp, S // tq, S // tk)
    return pl.pallas_call(
        _attn_kernel,
        grid=grid,
        in_specs=[
            pl.BlockSpec((1, hp, tq, HEAD_PAD), lambda b, h, i, j: (b, h, i, 0)),
            pl.BlockSpec((1, hp, tk, HEAD_PAD), lambda b, h, i, j: (b, h, j, 0)),
            pl.BlockSpec((1, hp, tk, HEAD_PAD), lambda b, h, i, j: (b, h, j, 0)),
        ],
        out_specs=pl.BlockSpec((1, tq, hp * V_HEAD_DIM), lambda b, h, i, j: (b, i, h)),
        out_shape=jax.ShapeDtypeStruct((B, S, H * V_HEAD_DIM), MXU_DTYPE),
        scratch_shapes=[pltpu.VMEM((hp, tq, HEAD_PAD), jnp.float32),
                        pltpu.VMEM((hp, tq, HEAD_PAD), jnp.float32)],
        compiler_params=_params("parallel", "parallel", "parallel", "arbitrary"),
        name="attention",
    )(q, k, v)


def _s5_kernel(u_ref, strip_ref, win_ref, wout_ref, dec_ref, y_ref, mt_ref, *, chunks_per_seq):
    L, Hg = SSM_CHUNK, SSM_GROUP
    for s in range(L):
        off = (L - 1 - s) * Hg
        mt_ref[s * Hg:(s + 1) * Hg, :] = strip_ref[0, :, off:off + L * Hg].astype(mt_ref.dtype)
    u = u_ref[0]
    y = _dot(u, mt_ref[...])
    loc = _dot(u, win_ref[0])
    nc = u.shape[0]
    half = SSM_STATE
    cidx = lax.broadcasted_iota(jnp.int32, (nc, 2 * half), 0) % chunks_per_seq
    xf = loc[:, :2 * half]
    xb = loc[:, 2 * half:]
    levels = chunks_per_seq.bit_length() - 1
    for lv in range(levels):
        sh = 1 << lv
        prev = jnp.where(cidx >= sh, pltpu.roll(xf, sh, 0), 0.0)
        xf = xf + prev * dec_ref[0, 4 * lv + 0:4 * lv + 1, :] \
            + pltpu.roll(prev, half, 1) * dec_ref[0, 4 * lv + 1:4 * lv + 2, :]
        nxt = jnp.where(cidx < chunks_per_seq - sh, pltpu.roll(xb, nc - sh, 0), 0.0)
        xb = xb + nxt * dec_ref[0, 4 * lv + 2:4 * lv + 3, :] \
            + pltpu.roll(nxt, half, 1) * dec_ref[0, 4 * lv + 3:4 * lv + 4, :]
    xin_f = jnp.where(cidx >= 1, pltpu.roll(xf, 1, 0), 0.0)
    xin_b = jnp.where(cidx < chunks_per_seq - 1, pltpu.roll(xb, nc - 1, 0), 0.0)
    xin = jnp.concatenate([xin_f, xin_b], axis=1).astype(MXU_DTYPE)
    y = y + _dot(xin, wout_ref[0])
    y_ref[0] = jax.nn.gelu(y).astype(y_ref.dtype)


def _s5(u_g, strip, win, wout, dec, chunks_per_seq):
    G, NC, W = u_g.shape
    kern = functools.partial(_s5_kernel, chunks_per_seq=chunks_per_seq)
    blk = lambda a: pl.BlockSpec((1,) + a.shape[1:], lambda g: (g, 0, 0))
    return pl.pallas_call(
        kern,
        grid=(G,),
        in_specs=[blk(u_g), blk(strip), blk(win), blk(wout), blk(dec)],
        out_specs=pl.BlockSpec((1, NC, W), lambda g: (g, 0, 0)),
        out_shape=jax.ShapeDtypeStruct((G, NC, W), MXU_DTYPE),
        scratch_shapes=[pltpu.VMEM((W, W), MXU_DTYPE)],
        compiler_params=_params("parallel"),
        name="s5_scan",
    )(u_g, strip, win, wout, dec)


def _cmul(ar, ai, br, bi):
    return ar * br - ai * bi, ar * bi + ai * br


def _s5_operators(lam_re, lam_im, log_dt, b_re, b_im, c_re, c_im, d_skip, chunks_per_seq):
    L, G, P, Hg = SSM_CHUNK, SSM_GROUPS, SSM_STATE, SSM_GROUP
    f32 = jnp.float32
    hi = lax.Precision.HIGHEST
    lr, li = lam_re.astype(f32), lam_im.astype(f32)
    dt = jnp.exp(log_dt.astype(f32))[..., None]
    ar, ai = lr * dt, li * dt

    def powers(n):
        mag = jnp.exp(ar[:, :, None, :] * n[None, None, :, None])
        ang = ai[:, :, None, :] * n[None, None, :, None]
        return mag * jnp.cos(ang), mag * jnp.sin(ang)

    pr, pi = powers(jnp.arange(L + 1, dtype=f32))
    nr, ni = pr[:, :, 1] - 1.0, pi[:, :, 1]
    den = lr * lr + li * li
    qr, qi = (nr * lr + ni * li) / den, (ni * lr - nr * li) / den
    bbr, bbi = _cmul(qr[..., None], qi[..., None], b_re.astype(f32), b_im.astype(f32))
    cr, ci = c_re.astype(f32), c_im.astype(f32)
    wr, wi = _cmul(cr[:, :, None], ci[:, :, None], pr[:, :, :L, None, :], pi[:, :, :L, None, :])
    kern = jnp.einsum('xgqj,xgdiq->xgjdi', jnp.concatenate([bbr, -bbi], axis=2),
                      jnp.concatenate([wr, wi], axis=-1), precision=hi)
    skip = jnp.eye(Hg, dtype=f32)[None, :, None, :] * d_skip.astype(f32).reshape(G, 1, 1, Hg)
    k0 = kern[0][:, :, :1] + kern[1][:, :, :1] + skip
    strip = jnp.concatenate([kern[1][:, :, :0:-1], k0, kern[0][:, :, 1:], jnp.zeros((G, Hg, 1, Hg), f32)], axis=2)
    strip = strip.reshape(G, Hg, 2 * L * Hg)
    to_in = lambda r, i: jnp.concatenate([r, i], axis=2).transpose(0, 1, 3, 2).reshape(G, L * Hg, 2 * P)
    wf = _cmul(pr[0][:, ::-1][:, 1:, :, None], pi[0][:, ::-1][:, 1:, :, None], bbr[0][:, None], bbi[0][:, None])
    wb = _cmul(pr[1][:, :L, :, None], pi[1][:, :L, :, None], bbr[1][:, None], bbi[1][:, None])
    win = jnp.concatenate([to_in(*wf), to_in(*wb)], axis=2)
    to_out = lambda r, i: jnp.concatenate([r, -i], axis=3).transpose(0, 3, 1, 2).reshape(G, 2 * P, L * Hg)
    of = _cmul(cr[0][:, None], ci[0][:, None], pr[0][:, 1:, None, :], pi[0][:, 1:, None, :])
    ob = _cmul(cr[1][:, None], ci[1][:, None], pr[1][:, ::-1][:, :L, None, :], pi[1][:, ::-1][:, :L, None, :])
    wout = jnp.concatenate([to_out(*of), to_out(*ob)], axis=1)
    levels = max(chunks_per_seq.bit_length() - 1, 1)
    dr, di = powers(L * (2.0 ** jnp.arange(levels, dtype=f32)))
    rows = jnp.stack([jnp.concatenate([dr, dr], -1), jnp.concatenate([-di, di], -1)], axis=3)
    dec = rows.transpose(1, 2, 0, 3, 4).reshape(G, levels * 4, 2 * P)
    return strip, win.astype(MXU_DTYPE), wout.astype(MXU_DTYPE), dec.astype(f32)


def _mix_kernel(x_ref, o_ref, gy_ref, gate_ref, woa_ref, wglu_ref, wos_ref, wout_ref, gffn_ref,
                x1_ref, h2t_ref):
    D = x_ref.shape[1]
    ya = _dot(o_ref[...], woa_ref[...])
    z = _dot(gy_ref[...], wglu_ref[...])
    yg = z[:, :SSM_WIDTH] * jax.nn.sigmoid(z[:, SSM_WIDTH:])
    ys = _dot(yg.astype(MXU_DTYPE), wos_ref[...])
    g = gate_ref[...].astype(jnp.float32)
    mixed = g[:, :D] * ya + g[:, D:] * ys
    x1 = x_ref[...] + _dot(mixed.astype(MXU_DTYPE), wout_ref[...])
    x1_ref[...] = x1
    h2t_ref[...] = _rms(x1, gffn_ref[...]).T.astype(h2t_ref.dtype)


def _mix(x2, o2, gy, gates, woa, wglu, wos, wout, gffn, tm):
    T, D = x2.shape
    row = lambda a: pl.BlockSpec((tm, a.shape[1]), lambda i: (i, 0))
    return pl.pallas_call(
        _mix_kernel,
        grid=(T // tm,),
        in_specs=[row(x2), row(o2), row(gy), row(gates),
                  _full(woa.shape), _full(wglu.shape), _full(wos.shape), _full(wout.shape), _full(gffn.shape)],
        out_specs=[pl.BlockSpec((tm, D), lambda i: (i, 0)), pl.BlockSpec((D, tm), lambda i: (0, i))],
        out_shape=[jax.ShapeDtypeStruct((T, D), jnp.float32), jax.ShapeDtypeStruct((D, T), MXU_DTYPE)],
        compiler_params=_params("parallel"),
        name="mix",
    )(x2, o2, gy, gates, woa, wglu, wos, wout, gffn)


def _extract_top(s, k, out_ref=None):
    kth = None
    rank = jnp.full(s.shape, float(k), jnp.float32)
    for i in range(k):
        kth = jnp.max(s, axis=0, keepdims=True)
        if out_ref is not None:
            out_ref[i:i + 1, :] = kth
        hit = s == kth
        rank = jnp.where(hit, float(i), rank)
        s = jnp.where(hit, -jnp.inf, s)
    return kth, rank


def _peer_stats_kernel(h2t_ref, wqt_ref, keys_ref, n1_ref, c_ref, r2_ref, e2_ref,
                       top1_ref, top2_ref, cand_ref):
    K = PEER_TOPK
    qpt = _dot(wqt_ref[...], h2t_ref[...])
    for hd in range(PEER_HEADS):
        r0 = hd * 2 * PEER_HALF
        s1 = _dot(keys_ref[2 * hd], qpt[r0:r0 + PEER_HALF].astype(MXU_DTYPE))
        s2 = _dot(keys_ref[2 * hd + 1], qpt[r0 + PEER_HALF:r0 + 2 * PEER_HALF].astype(MXU_DTYPE))
        _extract_top(s1, K, top1_ref)
        _, rank2 = _extract_top(s2, K, top2_ref)
        row = 0
        for lo, hi, keep in PEER_CAND_BLOCKS:
            for r1 in range(lo, hi):
                cand_ref[row:row + keep, :] = top1_ref[r1:r1 + 1, :] + top2_ref[0:keep, :]
                row += keep
        cand_ref[row:row + K // 2, :] = top1_ref[K // 2:K, :] + top2_ref[0:1, :]
        cand = cand_ref[...]
        tau, _ = _extract_top(cand, K)
        m1 = top1_ref[0:1, :]
        m2 = top2_ref[0:1, :]
        z = jnp.sum(jnp.where(cand >= tau, jnp.exp(cand - (m1 + m2)), 0.0), axis=0, keepdims=True)
        n1 = jnp.zeros(s1.shape, jnp.float32)
        for r in range(K // 2):
            n1 = n1 + jnp.where(s1 + top2_ref[r:r + 1, :] >= tau, 1.0, 0.0)
        extra = jnp.zeros(m1.shape, jnp.float32)
        for r in range(K // 2, K):
            extra = extra + jnp.where(m1 + top2_ref[r:r + 1, :] >= tau, 1.0, 0.0)
        n1_ref[hd] = n1 + jnp.where(s1 == m1, extra, 0.0)
        c_ref[hd] = jnp.exp(s1 - m1) * (0.5 / z)
        r2_ref[hd] = rank2.astype(r2_ref.dtype)
        e2_ref[hd] = jnp.exp(s2 - m2).astype(e2_ref.dtype)


def _peer_stats(h2t, wqt, keys, tm):
    D, T = h2t.shape
    st_spec = pl.BlockSpec((PEER_HEADS, N_KEYS, tm), lambda i: (0, 0, i))
    st = lambda dt: jax.ShapeDtypeStruct((PEER_HEADS, N_KEYS, T), dt)
    return pl.pallas_call(
        _peer_stats_kernel,
        grid=(T // tm,),
        in_specs=[pl.BlockSpec((D, tm), lambda i: (0, i)), _full(wqt.shape), _full(keys.shape)],
        out_specs=[st_spec] * 4,
        out_shape=[st(jnp.float32), st(jnp.float32), st(MXU_DTYPE), st(MXU_DTYPE)],
        scratch_shapes=[pltpu.VMEM((PEER_TOPK, tm), jnp.float32),
                        pltpu.VMEM((PEER_TOPK, tm), jnp.float32),
                        pltpu.VMEM((PEER_CAND_ROWS, tm), jnp.float32)],
        compiler_params=_params("parallel"),
        name="peer_stats",
    )(h2t, wqt, keys)


def _gelu_doubled(x):
    c0 = math.sqrt(2.0 / math.pi)
    inner = x * (c0 + (c0 * 0.044715) * (x * x))
    return x + x * jnp.tanh(inner)


def _peer_kernel(h2t_ref, wd_ref, wu_ref, n1_ref, c_ref, r2_ref, e2_ref, x1_ref, gfin_ref,
                 out_ref, acc_ref, act_ref):
    e = pl.program_id(1)

    @pl.when(e == 0)
    def _():
        acc_ref[...] = jnp.zeros(acc_ref.shape, jnp.float32)

    te, tm = wd_ref.shape[0], h2t_ref.shape[1]
    pk = BF16_SUBLANES
    nb = N_KEYS // pk
    blocks_per_sub = PEER_SUB // N_KEYS

    total = None
    for k in range(te // PEER_SUB):
        rows = slice(k * PEER_SUB, (k + 1) * PEER_SUB)
        a_t = _dot(wd_ref[rows, :], h2t_ref[...])
        for j in range(blocks_per_sub):
            i1 = (e * (te // PEER_SUB) + k) * blocks_per_sub + j
            gate = None
            for hd in range(PEER_HEADS):
                n1 = jnp.broadcast_to(n1_ref[hd, pl.ds(i1, 1), :], (pk, tm)).astype(MXU_DTYPE)
                cw = jnp.broadcast_to(c_ref[hd, pl.ds(i1, 1), :], (pk, tm)).astype(MXU_DTYPE)
                r2 = r2_ref[hd].reshape(nb, pk, tm)
                e2 = e2_ref[hd].reshape(nb, pk, tm)
                g = jnp.where(r2 < n1[None], e2, jnp.zeros_like(e2)) * cw[None]
                gate = g if gate is None else gate + g
            blk = a_t[j * N_KEYS:(j + 1) * N_KEYS].astype(MXU_DTYPE).reshape(nb, pk, tm)
            act = _gelu_doubled(blk) * gate
            act_ref[k * PEER_SUB + j * N_KEYS:k * PEER_SUB + (j + 1) * N_KEYS, :] = act.reshape(N_KEYS, tm)
        part = lax.dot_general(act_ref[rows, :], wu_ref[rows, :], _TN, preferred_element_type=jnp.float32)
        total = part if total is None else total + part
    acc_ref[...] += total

    @pl.when(e == pl.num_programs(1) - 1)
    def _():
        out_ref[...] = _rms(x1_ref[...] + acc_ref[...], gfin_ref[...])


def _peer(h2t, wd, wu, n1, cw, r2, e2, x1, gfin, tm, te):
    D, T = h2t.shape
    E = wd.shape[0]
    st_spec = pl.BlockSpec((PEER_HEADS, N_KEYS, tm), lambda i, e: (0, 0, i))
    return pl.pallas_call(
        _peer_kernel,
        grid=(T // tm, E // te),
        in_specs=[pl.BlockSpec((D, tm), lambda i, e: (0, i)),
                  pl.BlockSpec((te, D), lambda i, e: (e, 0)),
                  pl.BlockSpec((te, D), lambda i, e: (e, 0)),
                  st_spec, st_spec, st_spec, st_spec,
                  pl.BlockSpec((tm, D), lambda i, e: (i, 0)),
                  pl.BlockSpec(gfin.shape, lambda i, e: (0, 0))],
        out_specs=pl.BlockSpec((tm, D), lambda i, e: (i, 0)),
        out_shape=jax.ShapeDtypeStruct((T, D), jnp.float32),
        scratch_shapes=[pltpu.VMEM((tm, D), jnp.float32), pltpu.VMEM((te, tm), MXU_DTYPE)],
        compiler_params=_params("parallel", "arbitrary"),
        name="peer_experts",
    )(h2t, wd, wu, n1, cw, r2, e2, x1, gfin)


def _head_cols(w, per_head, start, width, dst):
    K = w.shape[0]
    blk = w.reshape(K, MLA_HEADS, per_head)[:, :, start:start + width]
    blk = jnp.pad(blk, ((0, 0), (0, 0), (dst, HEAD_PAD - dst - width)))
    return blk.reshape(K, MLA_HEADS * HEAD_PAD)


def _tile(n, pref):
    return pref if n % pref == 0 else n


def kernel(x, norm_mix, w_in, q_a_norm, w_q_b, kv_a_norm, w_kv_b, w_o_attn, lam_re, lam_im, log_dt, b_re, b_im, c_re, c_im, d_skip, w_glu, w_o_ssm, w_out, norm_ffn, w_query, sub_keys, w_down, w_up, final_norm):
    B, S, D = x.shape
    T = B * S
    f32 = jnp.float32
    cd = MXU_DTYPE
    half = QK_ROPE_DIM // 2
    l = 0

    pos = jnp.arange(S, dtype=f32)
    inv_freq = 1.0 / (ROPE_THETA ** (jnp.arange(0, QK_ROPE_DIM, 2, dtype=f32) / QK_ROPE_DIM))
    ang = pos[:, None] * inv_freq[None, :]
    cos, sin = jnp.cos(ang), jnp.sin(ang)
    pad = HEAD_PAD - QK_HEAD_DIM
    cos_t = jnp.concatenate([jnp.ones((S, QK_NOPE_DIM), f32), cos, cos, jnp.zeros((S, pad), f32)], axis=1)
    sin_t = jnp.concatenate([jnp.zeros((S, QK_NOPE_DIM), f32), -sin, sin, jnp.zeros((S, pad), f32)], axis=1)

    w = w_in[l]
    c0, c1, c2, c3 = Q_LORA_RANK, Q_LORA_RANK + KV_LORA_RANK, Q_LORA_RANK + KV_LORA_RANK + QK_ROPE_DIM, \
        Q_LORA_RANK + KV_LORA_RANK + QK_ROPE_DIM + SSM_WIDTH
    w_kr = w[:, c1:c2]
    zl = jnp.zeros((D, QK_NOPE_DIM), f32)
    zr = jnp.zeros((D, pad), f32)
    kr_plain = jnp.concatenate([zl, w_kr, zr], axis=1)
    kr_swap = jnp.concatenate([zl, w_kr[:, half:], w_kr[:, :half], zr], axis=1)
    wcat = jnp.concatenate([w[:, :c0], w[:, c0:c1], kr_plain, kr_swap, w[:, c2:c3], w[:, c3:]], axis=1).astype(cd)

    wq = w_q_b[l]
    wq_plain = _head_cols(wq, QK_HEAD_DIM, 0, QK_HEAD_DIM, 0)
    wq_swap = (_head_cols(wq, QK_HEAD_DIM, QK_NOPE_DIM + half, half, QK_NOPE_DIM)
               + _head_cols(wq, QK_HEAD_DIM, QK_NOPE_DIM, half, QK_NOPE_DIM + half))
    wq_cat = jnp.concatenate([wq_plain, wq_swap], axis=1).astype(cd)
    wkv = w_kv_b[l]
    wkv_cat = jnp.concatenate([_head_cols(wkv, QK_NOPE_DIM + V_HEAD_DIM, 0, QK_NOPE_DIM, 0),
                               _head_cols(wkv, QK_NOPE_DIM + V_HEAD_DIM, QK_NOPE_DIM, V_HEAD_DIM, 0)],
                              axis=1).astype(cd)

    tm = _tile(S, 256)
    q, k, v, u, gates = _inproj(x, cos_t, sin_t, norm_mix[l][None], wcat, q_a_norm[l][None], wq_cat,
                                kv_a_norm[l][None], wkv_cat, tm)

    o = _attention(q, k, v, _tile(S, 512), _tile(S, 1024))

    L, G, Hg = SSM_CHUNK, SSM_GROUPS, SSM_GROUP
    ncs = S // L
    strip, win, wout_s, dec = _s5_operators(lam_re[l], lam_im[l], log_dt[l], b_re[l], b_im[l], c_re[l], c_im[l],
                                         d_skip[l], ncs)
    u_g = u.reshape(B, ncs, L, G, Hg).transpose(3, 0, 1, 2, 4).reshape(G, B * ncs, L * Hg)
    gy_g = _s5(u_g, strip, win, wout_s, dec, ncs)
    gy = gy_g.reshape(G, B, ncs, L, Hg).transpose(1, 2, 3, 0, 4).reshape(T, SSM_WIDTH)

    x1, h2t = _mix(x.reshape(T, D), o.reshape(T, MLA_HEADS * V_HEAD_DIM), gy, gates,
                  w_o_attn[l].astype(cd), w_glu[l].astype(cd), w_o_ssm[l].astype(cd), w_out[l].astype(cd),
                  norm_ffn[l][None], _tile(T, 512))

    keys = sub_keys[l].reshape(PEER_HEADS * 2, N_KEYS, PEER_HALF).astype(cd)
    n1, cw, r2, e2 = _peer_stats(h2t, w_query[l].T.astype(cd), keys, _tile(T, 256))
    out = _peer(h2t, w_down[l].astype(cd), w_up[l].astype(cd), n1, cw, r2, e2, x1, final_norm[None],
                _tile(T, 512), 2048)
    return out.reshape(B, S, D)
```

```python
import functools
import math

import jax
import jax.numpy as jnp
from jax import lax
from jax.experimental import pallas as pl
from jax.experimental.pallas import tpu as pltpu

MLA_HEADS = 8
QK_NOPE_DIM = 64
QK_ROPE_DIM = 32
QK_HEAD_DIM = QK_NOPE_DIM + QK_ROPE_DIM
V_HEAD_DIM = 64
Q_LORA_RANK = 256
KV_LORA_RANK = 128
ROPE_THETA = 10000.0
SSM_WIDTH = 512
SSM_GROUP = 16
SSM_GROUPS = SSM_WIDTH // SSM_GROUP
SSM_STATE = 64
PEER_HEADS = 8
N_KEYS = 128
PEER_TOPK = 16
PEER_HALF = 64
EPS = 1e-6

HEAD_PAD = 128
BF16_SUBLANES = 16
PEER_SUB = 512
PEER_CAND_BLOCKS = ((0, 1, 16), (1, 4, 8), (4, 8, 4))
PEER_CAND_ROWS = sum((hi - lo) * keep for lo, hi, keep in PEER_CAND_BLOCKS) + PEER_TOPK // 2
ATTN_HEADS_PER_STEP = 8
ATTN_KEY_CHUNK = 2048
SSM_CHUNK = 64
MXU_DTYPE = jnp.bfloat16
VMEM_LIMIT_BYTES = 56 * 1024 * 1024

_NT = (((1,), (1,)), ((), ()))
_TN = (((0,), (0,)), ((), ()))


def _dot(a, b):
    return jnp.dot(a, b, preferred_element_type=jnp.float32)


def _rms(x, g):
    return x * lax.rsqrt(jnp.mean(x * x, axis=-1, keepdims=True) + EPS) * g


def _params(*sem):
    return pltpu.CompilerParams(dimension_semantics=sem, vmem_limit_bytes=VMEM_LIMIT_BYTES)


def _full(shape):
    return pl.BlockSpec(shape, lambda *_: (0,) * len(shape))


def _inproj_kernel(x_ref, cos_ref, sin_ref, gmix_ref, wcat_ref, gq_ref, wq_ref, gkv_ref, wkv_ref,
                   q_ref, k_ref, v_ref, u_ref, gate_ref):
    x = x_ref[0]
    h = _rms(x, gmix_ref[...]).astype(MXU_DTYPE)
    cos = cos_ref[...]
    sin = sin_ref[...]
    o = 0
    hq = _dot(h, wcat_ref[:, o:o + Q_LORA_RANK]); o += Q_LORA_RANK
    hkv = _dot(h, wcat_ref[:, o:o + KV_LORA_RANK]); o += KV_LORA_RANK
    kr = _dot(h, wcat_ref[:, o:o + 2 * HEAD_PAD]); o += 2 * HEAD_PAD
    u_ref[...] = _dot(h, wcat_ref[:, o:o + SSM_WIDTH]).astype(u_ref.dtype); o += SSM_WIDTH
    gate_ref[...] = jax.nn.sigmoid(_dot(h, wcat_ref[:, o:])).astype(gate_ref.dtype)

    nq = MLA_HEADS * HEAD_PAD
    hqn = _rms(hq, gq_ref[...]).astype(MXU_DTYPE)
    qq = _dot(hqn, wq_ref[...])
    hkvn = _rms(hkv, gkv_ref[...]).astype(MXU_DTYPE)
    kv = _dot(hkvn, wkv_ref[...])
    k_rope = kr[:, :HEAD_PAD] * cos + kr[:, HEAD_PAD:] * sin
    scale = QK_HEAD_DIM ** -0.5 * math.log2(math.e)
    lane = lax.broadcasted_iota(jnp.int32, (1, HEAD_PAD), 1)
    ones_pad = (lane >= V_HEAD_DIM).astype(jnp.float32)
    for hd in range(MLA_HEADS):
        sl = slice(hd * HEAD_PAD, (hd + 1) * HEAD_PAD)
        q = qq[:, sl] * cos + qq[:, nq + hd * HEAD_PAD: nq + (hd + 1) * HEAD_PAD] * sin
        q_ref[0, hd] = (q * scale).astype(q_ref.dtype)
        k_ref[0, hd] = (kv[:, sl] + k_rope).astype(k_ref.dtype)
        v_ref[0, hd] = (kv[:, nq + hd * HEAD_PAD: nq + (hd + 1) * HEAD_PAD] + ones_pad).astype(v_ref.dtype)


def _inproj(x, cos_t, sin_t, gmix, wcat, gq, wq, gkv, wkv, tm):
    B, S, D = x.shape
    H = MLA_HEADS
    ncat = wcat.shape[1]
    ngate = ncat - (Q_LORA_RANK + KV_LORA_RANK + 2 * HEAD_PAD + SSM_WIDTH)
    grid = (B, S // tm)
    hs = jax.ShapeDtypeStruct((B, H, S, HEAD_PAD), MXU_DTYPE)
    head_spec = pl.BlockSpec((1, H, tm, HEAD_PAD), lambda b, s: (b, 0, s, 0))
    return pl.pallas_call(
        _inproj_kernel,
        grid=grid,
        in_specs=[
            pl.BlockSpec((1, tm, D), lambda b, s: (b, s, 0)),
            pl.BlockSpec((tm, HEAD_PAD), lambda b, s: (s, 0)),
            pl.BlockSpec((tm, HEAD_PAD), lambda b, s: (s, 0)),
            _full(gmix.shape), _full(wcat.shape), _full(gq.shape), _full(wq.shape),
            _full(gkv.shape), _full(wkv.shape),
        ],
        out_specs=[
            head_spec, head_spec, head_spec,
            pl.BlockSpec((tm, SSM_WIDTH), lambda b, s, n=S // tm: (b * n + s, 0)),
            pl.BlockSpec((tm, ngate), lambda b, s, n=S // tm: (b * n + s, 0)),
        ],
        out_shape=[hs, hs, hs,
                   jax.ShapeDtypeStruct((B * S, SSM_WIDTH), MXU_DTYPE),
                   jax.ShapeDtypeStruct((B * S, ngate), MXU_DTYPE)],
        compiler_params=_params("parallel", "parallel"),
        name="inproj",
    )(x, cos_t, sin_t, gmix, wcat, gq, wq, gkv, wkv)


def _attn_kernel(q_ref, k_ref, v_ref, o_ref, m_ref, acc_ref):
    ki = pl.program_id(3)

    @pl.when(ki == 0)
    def _():
        m_ref[...] = jnp.full(m_ref.shape, -jnp.inf, jnp.float32)
        acc_ref[...] = jnp.zeros(acc_ref.shape, jnp.float32)

    tk = k_ref.shape[2]
    chunk = min(ATTN_KEY_CHUNK, tk)
    for j in range(ATTN_HEADS_PER_STEP):
        q = q_ref[0, j]
        m = m_ref[j]
        acc = acc_ref[j]
        for c in range(tk // chunk):
            keys = slice(c * chunk, (c + 1) * chunk)
            s = lax.dot_general(q, k_ref[0, j, keys, :], _NT, preferred_element_type=jnp.float32)
            m_new = jnp.maximum(m, jnp.max(s, axis=1, keepdims=True))
            p = jnp.exp2((s - jnp.tile(m_new, (1, chunk // HEAD_PAD))).astype(MXU_DTYPE))
            acc = jnp.exp2(m - m_new) * acc + _dot(p, v_ref[0, j, keys, :])
            m = m_new
        m_ref[j] = m
        acc_ref[j] = acc

    @pl.when(ki == pl.num_programs(3) - 1)
    def _():
        pairs = []
        for j in range(0, ATTN_HEADS_PER_STEP, 2):
            a0, a1 = acc_ref[j], acc_ref[j + 1]
            lane = lax.broadcasted_iota(jnp.int32, a0.shape, 1)
            o0 = a0 / pltpu.roll(a0, V_HEAD_DIM, 1)
            o1 = pltpu.roll(a1, V_HEAD_DIM, 1) / a1
            pairs.append(jnp.where(lane < V_HEAD_DIM, o0, o1))
        o_ref[0] = jnp.concatenate(pairs, axis=1).astype(o_ref.dtype)


def _attention(q, k, v, tq, tk):
    B, H, S, _ = q.shape
    hp = ATTN_HEADS_PER_STEP
    grid = (B, H // hp, S // tq, S // tk)
    return pl.pallas_call(
        _attn_kernel,
        grid=grid,
        in_specs=[
            pl.BlockSpec((1, hp, tq, HEAD_PAD), lambda b, h, i, j: (b, h, i, 0)),
            pl.BlockSpec((1, hp, tk, HEAD_PAD), lambda b, h, i, j: (b, h, j, 0)),
            pl.BlockSpec((1, hp, tk, HEAD_PAD), lambda b, h, i, j: (b, h, j, 0)),
        ],
        out_specs=pl.BlockSpec((1, tq, hp * V_HEAD_DIM), lambda b, h, i, j: (b, i, h)),
        out_shape=jax.ShapeDtypeStruct((B, S, H * V_HEAD_DIM), MXU_DTYPE),
        scratch_shapes=[pltpu.VMEM((hp, tq, HEAD_PAD), jnp.float32),
                        pltpu.VMEM((hp, tq, HEAD_PAD), jnp.float32)],
        compiler_params=_params("parallel", "parallel", "parallel", "arbitrary"),
        name="attention",
    )(q, k, v)


def _s5_kernel(u_ref, strip_ref, win_ref, wout_ref, dec_ref, y_ref, mt_ref, *, chunks_per_seq):
    L, Hg = SSM_CHUNK, SSM_GROUP
    for s in range(L):
        off = (L - 1 - s) * Hg
        mt_ref[s * Hg:(s + 1) * Hg, :] = strip_ref[0, :, off:off + L * Hg].astype(mt_ref.dtype)
    u = u_ref[0]
    y = _dot(u, mt_ref[...])
    loc = _dot(u, win_ref[0])
    nc = u.shape[0]
    half = SSM_STATE
    cidx = lax.broadcasted_iota(jnp.int32, (nc, 2 * half), 0) % chunks_per_seq
    xf = loc[:, :2 * half]
    xb = loc[:, 2 * half:]
    levels = chunks_per_seq.bit_length() - 1
    for lv in range(levels):
        sh = 1 << lv
        prev = jnp.where(cidx >= sh, pltpu.roll(xf, sh, 0), 0.0)
        xf = xf + prev * dec_ref[0, 4 * lv + 0:4 * lv + 1, :] \
            + pltpu.roll(prev, half, 1) * dec_ref[0, 4 * lv + 1:4 * lv + 2, :]
        nxt = jnp.where(cidx < chunks_per_seq - sh, pltpu.roll(xb, nc - sh, 0), 0.0)
        xb = xb + nxt * dec_ref[0, 4 * lv + 2:4 * lv + 3, :] \
            + pltpu.roll(nxt, half, 1) * dec_ref[0, 4 * lv + 3:4 * lv + 4, :]
    xin_f = jnp.where(cidx >= 1, pltpu.roll(xf, 1, 0), 0.0)
    xin_b = jnp.where(cidx < chunks_per_seq - 1, pltpu.roll(xb, nc - 1, 0), 0.0)
    xin = jnp.concatenate([xin_f, xin_b], axis=1).astype(MXU_DTYPE)
    y = y + _dot(xin, wout_ref[0])
    y_ref[0] = jax.nn.gelu(y).astype(y_ref.dtype)


def _s5(u_g, strip, win, wout, dec, chunks_per_seq):
    G, NC, W = u_g.shape
    kern = functools.partial(_s5_kernel, chunks_per_seq=chunks_per_seq)
    blk = lambda a: pl.BlockSpec((1,) + a.shape[1:], lambda g: (g, 0, 0))
    return pl.pallas_call(
        kern,
        grid=(G,),
        in_specs=[blk(u_g), blk(strip), blk(win), blk(wout), blk(dec)],
        out_specs=pl.BlockSpec((1, NC, W), lambda g: (g, 0, 0)),
        out_shape=jax.ShapeDtypeStruct((G, NC, W), MXU_DTYPE),
        scratch_shapes=[pltpu.VMEM((W, W), MXU_DTYPE)],
        compiler_params=_params("parallel"),
        name="s5_scan",
    )(u_g, strip, win, wout, dec)


def _cmul(ar, ai, br, bi):
    return ar * br - ai * bi, ar * bi + ai * br


def _s5_operators(lam_re, lam_im, log_dt, b_re, b_im, c_re, c_im, d_skip, chunks_per_seq):
    L, G, P, Hg = SSM_CHUNK, SSM_GROUPS, SSM_STATE, SSM_GROUP
    f32 = jnp.float32
    hi = lax.Precision.HIGHEST
    lr, li = lam_re.astype(f32), lam_im.astype(f32)
    dt = jnp.exp(log_dt.astype(f32))[..., None]
    ar, ai = lr * dt, li * dt

    def powers(n):
        mag = jnp.exp(ar[:, :, None, :] * n[None, None, :, None])
        ang = ai[:, :, None, :] * n[None, None, :, None]
        return mag * jnp.cos(ang), mag * jnp.sin(ang)

    pr, pi = powers(jnp.arange(L + 1, dtype=f32))
    nr, ni = pr[:, :, 1] - 1.0, pi[:, :, 1]
    den = lr * lr + li * li
    qr, qi = (nr * lr + ni * li) / den, (ni * lr - nr * li) / den
    bbr, bbi = _cmul(qr[..., None], qi[..., None], b_re.astype(f32), b_im.astype(f32))
    cr, ci = c_re.astype(f32), c_im.astype(f32)
    wr, wi = _cmul(cr[:, :, None], ci[:, :, None], pr[:, :, :L, None, :], pi[:, :, :L, None, :])
    kern = jnp.einsum('xgqj,xgdiq->xgjdi', jnp.concatenate([bbr, -bbi], axis=2),
                      jnp.concatenate([wr, wi], axis=-1), precision=hi)
    skip = jnp.eye(Hg, dtype=f32)[None, :, None, :] * d_skip.astype(f32).reshape(G, 1, 1, Hg)
    k0 = kern[0][:, :, :1] + kern[1][:, :, :1] + skip
    strip = jnp.concatenate([kern[1][:, :, :0:-1], k0, kern[0][:, :, 1:], jnp.zeros((G, Hg, 1, Hg), f32)], axis=2)
    strip = strip.reshape(G, Hg, 2 * L * Hg)
    to_in = lambda r, i: jnp.concatenate([r, i], axis=2).transpose(0, 1, 3, 2).reshape(G, L * Hg, 2 * P)
    wf = _cmul(pr[0][:, ::-1][:, 1:, :, None], pi[0][:, ::-1][:, 1:, :, None], bbr[0][:, None], bbi[0][:, None])
    wb = _cmul(pr[1][:, :L, :, None], pi[1][:, :L, :, None], bbr[1][:, None], bbi[1][:, None])
    win = jnp.concatenate([to_in(*wf), to_in(*wb)], axis=2)
    to_out = lambda r, i: jnp.concatenate([r, -i], axis=3).transpose(0, 3, 1, 2).reshape(G, 2 * P, L * Hg)
    of = _cmul(cr[0][:, None], ci[0][:, None], pr[0][:, 1:, None, :], pi[0][:, 1:, None, :])
    ob = _cmul(cr[1][:, None], ci[1][:, None], pr[1][:, ::-1][:, :L, None, :], pi[1][:, ::-1][:, :L, None, :])
    wout = jnp.concatenate([to_out(*of), to_out(*ob)], axis=1)
    levels = max(chunks_per_seq.bit_length() - 1, 1)
    dr, di = powers(L * (2.0 ** jnp.arange(levels, dtype=f32)))
    rows = jnp.stack([jnp.concatenate([dr, dr], -1), jnp.concatenate([-di, di], -1)], axis=3)
    dec = rows.transpose(1, 2, 0, 3, 4).reshape(G, levels * 4, 2 * P)
    return strip, win.astype(MXU_DTYPE), wout.astype(MXU_DTYPE), dec.astype(f32)


def _mix_kernel(x_ref, o_ref, gy_ref, gate_ref, woa_ref, wglu_ref, wos_ref, wout_ref, gffn_ref,
                x1_ref, h2t_ref):
    D = x_ref.shape[1]
    ya = _dot(o_ref[...], woa_ref[...])
    z = _dot(gy_ref[...], wglu_ref[...])
    yg = z[:, :SSM_WIDTH] * jax.nn.sigmoid(z[:, SSM_WIDTH:])
    ys = _dot(yg.astype(MXU_DTYPE), wos_ref[...])
    g = gate_ref[...].astype(jnp.float32)
    mixed = g[:, :D] * ya + g[:, D:] * ys
    x1 = x_ref[...] + _dot(mixed.astype(MXU_DTYPE), wout_ref[...])
    x1_ref[...] = x1
    h2t_ref[...] = _rms(x1, gffn_ref[...]).T.astype(h2t_ref.dtype)


def _mix(x2, o2, gy, gates, woa, wglu, wos, wout, gffn, tm):
    T, D = x2.shape
    row = lambda a: pl.BlockSpec((tm, a.shape[1]), lambda i: (i, 0))
    return pl.pallas_call(
        _mix_kernel,
        grid=(T // tm,),
        in_specs=[row(x2), row(o2), row(gy), row(gates),
                  _full(woa.shape), _full(wglu.shape), _full(wos.shape), _full(wout.shape), _full(gffn.shape)],
        out_specs=[pl.BlockSpec((tm, D), lambda i: (i, 0)), pl.BlockSpec((D, tm), lambda i: (0, i))],
        out_shape=[jax.ShapeDtypeStruct((T, D), jnp.float32), jax.ShapeDtypeStruct((D, T), MXU_DTYPE)],
        compiler_params=_params("parallel"),
        name="mix",
    )(x2, o2, gy, gates, woa, wglu, wos, wout, gffn)


def _extract_top(s, k, out_ref=None):
    kth = None
    rank = jnp.full(s.shape, float(k), jnp.float32)
    for i in range(k):
        kth = jnp.max(s, axis=0, keepdims=True)
        if out_ref is not None:
            out_ref[i:i + 1, :] = kth
        hit = s == kth
        rank = jnp.where(hit, float(i), rank)
        s = jnp.where(hit, -jnp.inf, s)
    return kth, rank


def _peer_stats_kernel(h2t_ref, wqt_ref, keys_ref, n1_ref, c_ref, r2_ref, e2_ref,
                       top1_ref, top2_ref, cand_ref):
    K = PEER_TOPK
    qpt = _dot(wqt_ref[...], h2t_ref[...])
    for hd in range(PEER_HEADS):
        r0 = hd * 2 * PEER_HALF
        s1 = _dot(keys_ref[2 * hd], qpt[r0:r0 + PEER_HALF].astype(MXU_DTYPE))
        s2 = _dot(keys_ref[2 * hd + 1], qpt[r0 + PEER_HALF:r0 + 2 * PEER_HALF].astype(MXU_DTYPE))
        _extract_top(s1, K, top1_ref)
        _, rank2 = _extract_top(s2, K, top2_ref)
        row = 0
        for lo, hi, keep in PEER_CAND_BLOCKS:
            for r1 in range(lo, hi):
                cand_ref[row:row + keep, :] = top1_ref[r1:r1 + 1, :] + top2_ref[0:keep, :]
                row += keep
        cand_ref[row:row + K // 2, :] = top1_ref[K // 2:K, :] + top2_ref[0:1, :]
        cand = cand_ref[...]
        tau, _ = _extract_top(cand, K)
        m1 = top1_ref[0:1, :]
        m2 = top2_ref[0:1, :]
        z = jnp.sum(jnp.where(cand >= tau, jnp.exp(cand - (m1 + m2)), 0.0), axis=0, keepdims=True)
        n1 = jnp.zeros(s1.shape, jnp.float32)
        for r in range(K // 2):
            n1 = n1 + jnp.where(s1 + top2_ref[r:r + 1, :] >= tau, 1.0, 0.0)
        extra = jnp.zeros(m1.shape, jnp.float32)
        for r in range(K // 2, K):
            extra = extra + jnp.where(m1 + top2_ref[r:r + 1, :] >= tau, 1.0, 0.0)
        n1_ref[hd] = n1 + jnp.where(s1 == m1, extra, 0.0)
        c_ref[hd] = jnp.exp(s1 - m1) * (0.5 / z)
        r2_ref[hd] = rank2.astype(r2_ref.dtype)
        e2_ref[hd] = jnp.exp(s2 - m2).astype(e2_ref.dtype)


def _peer_stats(h2t, wqt, keys, tm):
    D, T = h2t.shape
    st_spec = pl.BlockSpec((PEER_HEADS, N_KEYS, tm), lambda i: (0, 0, i))
    st = lambda dt: jax.ShapeDtypeStruct((PEER_HEADS, N_KEYS, T), dt)
    return pl.pallas_call(
        _peer_stats_kernel,
        grid=(T // tm,),
        in_specs=[pl.BlockSpec((D, tm), lambda i: (0, i)), _full(wqt.shape), _full(keys.shape)],
        out_specs=[st_spec] * 4,
        out_shape=[st(jnp.float32), st(jnp.float32), st(MXU_DTYPE), st(MXU_DTYPE)],
        scratch_shapes=[pltpu.VMEM((PEER_TOPK, tm), jnp.float32),
                        pltpu.VMEM((PEER_TOPK, tm), jnp.float32),
                        pltpu.VMEM((PEER_CAND_ROWS, tm), jnp.float32)],
        compiler_params=_params("parallel"),
        name="peer_stats",
    )(h2t, wqt, keys)


def _gelu_doubled(x):
    c0 = math.sqrt(2.0 / math.pi)
    inner = x * (c0 + (c0 * 0.044715) * (x * x))
    return x + x * jnp.tanh(inner)


def _peer_kernel(h2t_ref, wd_ref, wu_ref, n1_ref, c_ref, r2_ref, e2_ref, x1_ref, gfin_ref,
                 out_ref, acc_ref, act_ref):
    e = pl.program_id(1)

    @pl.when(e == 0)
    def _():
        acc_ref[...] = jnp.zeros(acc_ref.shape, jnp.float32)

    te, tm = wd_ref.shape[0], h2t_ref.shape[1]
    pk = BF16_SUBLANES
    nb = N_KEYS // pk
    blocks_per_sub = PEER_SUB // N_KEYS

    total = None
    for k in range(te // PEER_SUB):
        rows = slice(k * PEER_SUB, (k + 1) * PEER_SUB)
        a_t = _dot(wd_ref[rows, :], h2t_ref[...])
        for j in range(blocks_per_sub):
            i1 = (e * (te // PEER_SUB) + k) * blocks_per_sub + j
            gate = None
            for hd in range(PEER_HEADS):
                n1 = jnp.broadcast_to(n1_ref[hd, pl.ds(i1, 1), :], (pk, tm)).astype(MXU_DTYPE)
                cw = jnp.broadcast_to(c_ref[hd, pl.ds(i1, 1), :], (pk, tm)).astype(MXU_DTYPE)
                r2 = r2_ref[hd].reshape(nb, pk, tm)
                e2 = e2_ref[hd].reshape(nb, pk, tm)
                g = jnp.where(r2 < n1[None], e2, jnp.zeros_like(e2)) * cw[None]
                gate = g if gate is None else gate + g
            blk = a_t[j * N_KEYS:(j + 1) * N_KEYS].astype(MXU_DTYPE).reshape(nb, pk, tm)
            act = _gelu_doubled(blk) * gate
            act_ref[k * PEER_SUB + j * N_KEYS:k * PEER_SUB + (j + 1) * N_KEYS, :] = act.reshape(N_KEYS, tm)
        part = lax.dot_general(act_ref[rows, :], wu_ref[rows, :], _TN, preferred_element_type=jnp.float32)
        total = part if total is None else total + part
    acc_ref[...] += total

    @pl.when(e == pl.num_programs(1) - 1)
    def _():
        out_ref[...] = _rms(x1_ref[...] + acc_ref[...], gfin_ref[...])


def _peer(h2t, wd, wu, n1, cw, r2, e2, x1, gfin, tm, te):
    D, T = h2t.shape
    E = wd.shape[0]
    st_spec = pl.BlockSpec((PEER_HEADS, N_KEYS, tm), lambda i, e: (0, 0, i))
    return pl.pallas_call(
        _peer_kernel,
        grid=(T // tm, E // te),
        in_specs=[pl.BlockSpec((D, tm), lambda i, e: (0, i)),
                  pl.BlockSpec((te, D), lambda i, e: (e, 0)),
                  pl.BlockSpec((te, D), lambda i, e: (e, 0)),
                  st_spec, st_spec, st_spec, st_spec,
                  pl.BlockSpec((tm, D), lambda i, e: (i, 0)),
                  pl.BlockSpec(gfin.shape, lambda i, e: (0, 0))],
        out_specs=pl.BlockSpec((tm, D), lambda i, e: (i, 0)),
        out_shape=jax.ShapeDtypeStruct((T, D), jnp.float32),
        scratch_shapes=[pltpu.VMEM((tm, D), jnp.float32), pltpu.VMEM((te, tm), MXU_DTYPE)],
        compiler_params=_params("parallel", "arbitrary"),
        name="peer_experts",
    )(h2t, wd, wu, n1, cw, r2, e2, x1, gfin)


def _head_cols(w, per_head, start, width, dst):
    K = w.shape[0]
    blk = w.reshape(K, MLA_HEADS, per_head)[:, :, start:start + width]
    blk = jnp.pad(blk, ((0, 0), (0, 0), (dst, HEAD_PAD - dst - width)))
    return blk.reshape(K, MLA_HEADS * HEAD_PAD)


def _tile(n, pref):
    return pref if n % pref == 0 else n


def kernel(x, norm_mix, w_in, q_a_norm, w_q_b, kv_a_norm, w_kv_b, w_o_attn, lam_re, lam_im, log_dt, b_re, b_im, c_re, c_im, d_skip, w_glu, w_o_ssm, w_out, norm_ffn, w_query, sub_keys, w_down, w_up, final_norm):
    B, S, D = x.shape
    T = B * S
    f32 = jnp.float32
    cd = MXU_DTYPE
    half = QK_ROPE_DIM // 2
    l = 0

    pos = jnp.arange(S, dtype=f32)
    inv_freq = 1.0 / (ROPE_THETA ** (jnp.arange(0, QK_ROPE_DIM, 2, dtype=f32) / QK_ROPE_DIM))
    ang = pos[:, None] * inv_freq[None, :]
    cos, sin = jnp.cos(ang), jnp.sin(ang)
    pad = HEAD_PAD - QK_HEAD_DIM
    cos_t = jnp.concatenate([jnp.ones((S, QK_NOPE_DIM), f32), cos, cos, jnp.zeros((S, pad), f32)], axis=1)
    sin_t = jnp.concatenate([jnp.zeros((S, QK_NOPE_DIM), f32), -sin, sin, jnp.zeros((S, pad), f32)], axis=1)

    w = w_in[l]
    c0, c1, c2, c3 = Q_LORA_RANK, Q_LORA_RANK + KV_LORA_RANK, Q_LORA_RANK + KV_LORA_RANK + QK_ROPE_DIM, \
        Q_LORA_RANK + KV_LORA_RANK + QK_ROPE_DIM + SSM_WIDTH
    w_kr = w[:, c1:c2]
    zl = jnp.zeros((D, QK_NOPE_DIM), f32)
    zr = jnp.zeros((D, pad), f32)
    kr_plain = jnp.concatenate([zl, w_kr, zr], axis=1)
    kr_swap = jnp.concatenate([zl, w_kr[:, half:], w_kr[:, :half], zr], axis=1)
    wcat = jnp.concatenate([w[:, :c0], w[:, c0:c1], kr_plain, kr_swap, w[:, c2:c3], w[:, c3:]], axis=1).astype(cd)

    wq = w_q_b[l]
    wq_plain = _head_cols(wq, QK_HEAD_DIM, 0, QK_HEAD_DIM, 0)
    wq_swap = (_head_cols(wq, QK_HEAD_DIM, QK_NOPE_DIM + half, half, QK_NOPE_DIM)
               + _head_cols(wq, QK_HEAD_DIM, QK_NOPE_DIM, half, QK_NOPE_DIM + half))
    wq_cat = jnp.concatenate([wq_plain, wq_swap], axis=1).astype(cd)
    wkv = w_kv_b[l]
    wkv_cat = jnp.concatenate([_head_cols(wkv, QK_NOPE_DIM + V_HEAD_DIM, 0, QK_NOPE_DIM, 0),
                               _head_cols(wkv, QK_NOPE_DIM + V_HEAD_DIM, QK_NOPE_DIM, V_HEAD_DIM, 0)],
                              axis=1).astype(cd)

    tm = _tile(S, 256)
    q, k, v, u, gates = _inproj(x, cos_t, sin_t, norm_mix[l][None], wcat, q_a_norm[l][None], wq_cat,
                                kv_a_norm[l][None], wkv_cat, tm)

    o = _attention(q, k, v, _tile(S, 512), _tile(S, 2048))

    L, G, Hg = SSM_CHUNK, SSM_GROUPS, SSM_GROUP
    ncs = S // L
    strip, win, wout_s, dec = _s5_operators(lam_re[l], lam_im[l], log_dt[l], b_re[l], b_im[l], c_re[l], c_im[l],
                                         d_skip[l], ncs)
    u_g = u.reshape(B, ncs, L, G, Hg).transpose(3, 0, 1, 2, 4).reshape(G, B * ncs, L * Hg)
    gy_g = _s5(u_g, strip, win, wout_s, dec, ncs)
    gy = gy_g.reshape(G, B, ncs, L, Hg).transpose(1, 2, 3, 0, 4).reshape(T, SSM_WIDTH)

    x1, h2t = _mix(x.reshape(T, D), o.reshape(T, MLA_HEADS * V_HEAD_DIM), gy, gates,
                  w_o_attn[l].astype(cd), w_glu[l].astype(cd), w_o_ssm[l].astype(cd), w_out[l].astype(cd),
                  norm_ffn[l][None], _tile(T, 512))

    keys = sub_keys[l].reshape(PEER_HEADS * 2, N_KEYS, PEER_HALF).astype(cd)
    n1, cw, r2, e2 = _peer_stats(h2t, w_query[l].T.astype(cd), keys, _tile(T, 256))
    out = _peer(h2t, w_down[l].astype(cd), w_up[l].astype(cd), n1, cw, r2, e2, x1, final_norm[None],
                _tile(T, 512), 2048)
    return out.reshape(B, S, D)
```

```python
import functools
import math

import jax
import jax.numpy as jnp
from jax import lax
from jax.experimental import pallas as pl
from jax.experimental.pallas import tpu as pltpu

MLA_HEADS = 8
QK_NOPE_DIM = 64
QK_ROPE_DIM = 32
QK_HEAD_DIM = QK_NOPE_DIM + QK_ROPE_DIM
V_HEAD_DIM = 64
Q_LORA_RANK = 256
KV_LORA_RANK = 128
ROPE_THETA = 10000.0
SSM_WIDTH = 512
SSM_GROUP = 16
SSM_GROUPS = SSM_WIDTH // SSM_GROUP
SSM_STATE = 64
PEER_HEADS = 8
N_KEYS = 128
PEER_TOPK = 16
PEER_HALF = 64
EPS = 1e-6
GELU_C0 = math.sqrt(2.0 / math.pi)
GELU_C1 = 0.044715

HEAD_PAD = 128
BF16_SUBLANES = 16
PEER_SUB = 512
PEER_CAND_BLOCKS = ((0, 1, 16), (1, 4, 8), (4, 8, 4))
PEER_CAND_ROWS = sum((hi - lo) * keep for lo, hi, keep in PEER_CAND_BLOCKS) + PEER_TOPK // 2
ATTN_HEADS_PER_STEP = 8
ATTN_KEY_CHUNK = 2048
SSM_CHUNK = 64
MXU_DTYPE = jnp.bfloat16
VMEM_LIMIT_BYTES = 56 * 1024 * 1024

_NT = (((1,), (1,)), ((), ()))
_TN = (((0,), (0,)), ((), ()))


def _dot(a, b):
    return jnp.dot(a, b, preferred_element_type=jnp.float32)


def _rms(x, g):
    return x * lax.rsqrt(jnp.mean(x * x, axis=-1, keepdims=True) + EPS) * g


def _params(*sem):
    return pltpu.CompilerParams(dimension_semantics=sem, vmem_limit_bytes=VMEM_LIMIT_BYTES)


def _full(shape):
    return pl.BlockSpec(shape, lambda *_: (0,) * len(shape))


def _inproj_kernel(x_ref, cos_ref, sin_ref, gmix_ref, wcat_ref, gq_ref, wq_ref, gkv_ref, wkv_ref,
                   q_ref, k_ref, v_ref, u_ref, gate_ref):
    x = x_ref[0]
    h = _rms(x, gmix_ref[...]).astype(MXU_DTYPE)
    cos = cos_ref[...]
    sin = sin_ref[...]
    o = 0
    hq = _dot(h, wcat_ref[:, o:o + Q_LORA_RANK]); o += Q_LORA_RANK
    hkv = _dot(h, wcat_ref[:, o:o + KV_LORA_RANK]); o += KV_LORA_RANK
    kr = _dot(h, wcat_ref[:, o:o + 2 * HEAD_PAD]); o += 2 * HEAD_PAD
    u_ref[...] = _dot(h, wcat_ref[:, o:o + SSM_WIDTH]).astype(u_ref.dtype); o += SSM_WIDTH
    gate_ref[...] = jax.nn.sigmoid(_dot(h, wcat_ref[:, o:])).astype(gate_ref.dtype)

    nq = MLA_HEADS * HEAD_PAD
    hqn = _rms(hq, gq_ref[...]).astype(MXU_DTYPE)
    qq = _dot(hqn, wq_ref[...])
    hkvn = _rms(hkv, gkv_ref[...]).astype(MXU_DTYPE)
    kv = _dot(hkvn, wkv_ref[...])
    k_rope = kr[:, :HEAD_PAD] * cos + kr[:, HEAD_PAD:] * sin
    scale = QK_HEAD_DIM ** -0.5 * math.log2(math.e)
    lane = lax.broadcasted_iota(jnp.int32, (1, HEAD_PAD), 1)
    ones_pad = (lane >= V_HEAD_DIM).astype(jnp.float32)
    for hd in range(MLA_HEADS):
        sl = slice(hd * HEAD_PAD, (hd + 1) * HEAD_PAD)
        q = qq[:, sl] * cos + qq[:, nq + hd * HEAD_PAD: nq + (hd + 1) * HEAD_PAD] * sin
        q_ref[0, hd] = (q * scale).astype(q_ref.dtype)
        k_ref[0, hd] = (kv[:, sl] + k_rope).astype(k_ref.dtype)
        v_ref[0, hd] = (kv[:, nq + hd * HEAD_PAD: nq + (hd + 1) * HEAD_PAD] + ones_pad).astype(v_ref.dtype)


def _inproj(x, cos_t, sin_t, gmix, wcat, gq, wq, gkv, wkv, tm):
    B, S, D = x.shape
    H = MLA_HEADS
    ncat = wcat.shape[1]
    ngate = ncat - (Q_LORA_RANK + KV_LORA_RANK + 2 * HEAD_PAD + SSM_WIDTH)
    grid = (B, S // tm)
    hs = jax.ShapeDtypeStruct((B, H, S, HEAD_PAD), MXU_DTYPE)
    head_spec = pl.BlockSpec((1, H, tm, HEAD_PAD), lambda b, s: (b, 0, s, 0))
    return pl.pallas_call(
        _inproj_kernel,
        grid=grid,
        in_specs=[
            pl.BlockSpec((1, tm, D), lambda b, s: (b, s, 0)),
            pl.BlockSpec((tm, HEAD_PAD), lambda b, s: (s, 0)),
            pl.BlockSpec((tm, HEAD_PAD), lambda b, s: (s, 0)),
            _full(gmix.shape), _full(wcat.shape), _full(gq.shape), _full(wq.shape),
            _full(gkv.shape), _full(wkv.shape),
        ],
        out_specs=[
            head_spec, head_spec, head_spec,
            pl.BlockSpec((tm, SSM_WIDTH), lambda b, s, n=S // tm: (b * n + s, 0)),
            pl.BlockSpec((tm, ngate), lambda b, s, n=S // tm: (b * n + s, 0)),
        ],
        out_shape=[hs, hs, hs,
                   jax.ShapeDtypeStruct((B * S, SSM_WIDTH), MXU_DTYPE),
                   jax.ShapeDtypeStruct((B * S, ngate), MXU_DTYPE)],
        compiler_params=_params("parallel", "parallel"),
        name="inproj",
    )(x, cos_t, sin_t, gmix, wcat, gq, wq, gkv, wkv)


def _attn_kernel(q_ref, k_ref, v_ref, o_ref, m_ref, acc_ref):
    ki = pl.program_id(3)

    @pl.when(ki == 0)
    def _():
        m_ref[...] = jnp.full(m_ref.shape, -jnp.inf, jnp.float32)
        acc_ref[...] = jnp.zeros(acc_ref.shape, jnp.float32)

    tk = k_ref.shape[2]
    chunk = min(ATTN_KEY_CHUNK, tk)
    for j in range(ATTN_HEADS_PER_STEP):
        q = q_ref[0, j]
        m = m_ref[j]
        acc = acc_ref[j]
        for c in range(tk // chunk):
            keys = slice(c * chunk, (c + 1) * chunk)
            s = lax.dot_general(q, k_ref[0, j, keys, :], _NT, preferred_element_type=jnp.float32)
            m_new = jnp.maximum(m, jnp.max(s, axis=1, keepdims=True))
            p = jnp.exp2((s - jnp.tile(m_new, (1, chunk // HEAD_PAD))).astype(MXU_DTYPE))
            acc = jnp.exp2(m - m_new) * acc + _dot(p, v_ref[0, j, keys, :])
            m = m_new
        m_ref[j] = m
        acc_ref[j] = acc

    @pl.when(ki == pl.num_programs(3) - 1)
    def _():
        pairs = []
        for j in range(0, ATTN_HEADS_PER_STEP, 2):
            a0, a1 = acc_ref[j], acc_ref[j + 1]
            lane = lax.broadcasted_iota(jnp.int32, a0.shape, 1)
            o0 = a0 / pltpu.roll(a0, V_HEAD_DIM, 1)
            o1 = pltpu.roll(a1, V_HEAD_DIM, 1) / a1
            pairs.append(jnp.where(lane < V_HEAD_DIM, o0, o1))
        o_ref[0] = jnp.concatenate(pairs, axis=1).astype(o_ref.dtype)


def _attention(q, k, v, tq, tk):
    B, H, S, _ = q.shape
    hp = ATTN_HEADS_PER_STEP
    grid = (B, H // hp, S // tq, S // tk)
    return pl.pallas_call(
        _attn_kernel,
        grid=grid,
        in_specs=[
            pl.BlockSpec((1, hp, tq, HEAD_PAD), lambda b, h, i, j: (b, h, i, 0)),
            pl.BlockSpec((1, hp, tk, HEAD_PAD), lambda b, h, i, j: (b, h, j, 0)),
            pl.BlockSpec((1, hp, tk, HEAD_PAD), lambda b, h, i, j: (b, h, j, 0)),
        ],
        out_specs=pl.BlockSpec((1, tq, hp * V_HEAD_DIM), lambda b, h, i, j: (b, i, h)),
        out_shape=jax.ShapeDtypeStruct((B, S, H * V_HEAD_DIM), MXU_DTYPE),
        scratch_shapes=[pltpu.VMEM((hp, tq, HEAD_PAD), jnp.float32),
                        pltpu.VMEM((hp, tq, HEAD_PAD), jnp.float32)],
        compiler_params=_params("parallel", "parallel", "parallel", "arbitrary"),
        name="attention",
    )(q, k, v)


def _s5_kernel(u_ref, strip_ref, win_ref, wout_ref, dec_ref, y_ref, mt_ref, *, chunks_per_seq):
    L, Hg = SSM_CHUNK, SSM_GROUP
    for s in range(L):
        off = (L - 1 - s) * Hg
        mt_ref[s * Hg:(s + 1) * Hg, :] = strip_ref[0, :, off:off + L * Hg].astype(mt_ref.dtype)
    u = u_ref[0]
    y = _dot(u, mt_ref[...])
    loc = _dot(u, win_ref[0])
    nc = u.shape[0]
    half = SSM_STATE
    cidx = lax.broadcasted_iota(jnp.int32, (nc, 2 * half), 0) % chunks_per_seq
    xf = loc[:, :2 * half]
    xb = loc[:, 2 * half:]
    levels = chunks_per_seq.bit_length() - 1
    for lv in range(levels):
        sh = 1 << lv
        prev = jnp.where(cidx >= sh, pltpu.roll(xf, sh, 0), 0.0)
        xf = xf + prev * dec_ref[0, 4 * lv + 0:4 * lv + 1, :] \
            + pltpu.roll(prev, half, 1) * dec_ref[0, 4 * lv + 1:4 * lv + 2, :]
        nxt = jnp.where(cidx < chunks_per_seq - sh, pltpu.roll(xb, nc - sh, 0), 0.0)
        xb = xb + nxt * dec_ref[0, 4 * lv + 2:4 * lv + 3, :] \
            + pltpu.roll(nxt, half, 1) * dec_ref[0, 4 * lv + 3:4 * lv + 4, :]
    xin_f = jnp.where(cidx >= 1, pltpu.roll(xf, 1, 0), 0.0)
    xin_b = jnp.where(cidx < chunks_per_seq - 1, pltpu.roll(xb, nc - 1, 0), 0.0)
    xin = jnp.concatenate([xin_f, xin_b], axis=1).astype(MXU_DTYPE)
    y = y + _dot(xin, wout_ref[0])
    y_ref[0] = jax.nn.gelu(y).astype(y_ref.dtype)


def _s5(u_g, strip, win, wout, dec, chunks_per_seq):
    G, NC, W = u_g.shape
    kern = functools.partial(_s5_kernel, chunks_per_seq=chunks_per_seq)
    blk = lambda a: pl.BlockSpec((1,) + a.shape[1:], lambda g: (g, 0, 0))
    return pl.pallas_call(
        kern,
        grid=(G,),
        in_specs=[blk(u_g), blk(strip), blk(win), blk(wout), blk(dec)],
        out_specs=pl.BlockSpec((1, NC, W), lambda g: (g, 0, 0)),
        out_shape=jax.ShapeDtypeStruct((G, NC, W), MXU_DTYPE),
        scratch_shapes=[pltpu.VMEM((W, W), MXU_DTYPE)],
        compiler_params=_params("parallel"),
        name="s5_scan",
    )(u_g, strip, win, wout, dec)


def _cmul(ar, ai, br, bi):
    return ar * br - ai * bi, ar * bi + ai * br


def _s5_operators(lam_re, lam_im, log_dt, b_re, b_im, c_re, c_im, d_skip, chunks_per_seq):
    L, G, P, Hg = SSM_CHUNK, SSM_GROUPS, SSM_STATE, SSM_GROUP
    f32 = jnp.float32
    hi = lax.Precision.HIGHEST
    lr, li = lam_re.astype(f32), lam_im.astype(f32)
    dt = jnp.exp(log_dt.astype(f32))[..., None]
    ar, ai = lr * dt, li * dt

    def powers(n):
        mag = jnp.exp(ar[:, :, None, :] * n[None, None, :, None])
        ang = ai[:, :, None, :] * n[None, None, :, None]
        return mag * jnp.cos(ang), mag * jnp.sin(ang)

    pr, pi = powers(jnp.arange(L + 1, dtype=f32))
    nr, ni = pr[:, :, 1] - 1.0, pi[:, :, 1]
    den = lr * lr + li * li
    qr, qi = (nr * lr + ni * li) / den, (ni * lr - nr * li) / den
    bbr, bbi = _cmul(qr[..., None], qi[..., None], b_re.astype(f32), b_im.astype(f32))
    cr, ci = c_re.astype(f32), c_im.astype(f32)
    wr, wi = _cmul(cr[:, :, None], ci[:, :, None], pr[:, :, :L, None, :], pi[:, :, :L, None, :])
    kern = jnp.einsum('xgqj,xgdiq->xgjdi', jnp.concatenate([bbr, -bbi], axis=2),
                      jnp.concatenate([wr, wi], axis=-1), precision=hi)
    skip = jnp.eye(Hg, dtype=f32)[None, :, None, :] * d_skip.astype(f32).reshape(G, 1, 1, Hg)
    k0 = kern[0][:, :, :1] + kern[1][:, :, :1] + skip
    strip = jnp.concatenate([kern[1][:, :, :0:-1], k0, kern[0][:, :, 1:], jnp.zeros((G, Hg, 1, Hg), f32)], axis=2)
    strip = strip.reshape(G, Hg, 2 * L * Hg)
    to_in = lambda r, i: jnp.concatenate([r, i], axis=2).transpose(0, 1, 3, 2).reshape(G, L * Hg, 2 * P)
    wf = _cmul(pr[0][:, ::-1][:, 1:, :, None], pi[0][:, ::-1][:, 1:, :, None], bbr[0][:, None], bbi[0][:, None])
    wb = _cmul(pr[1][:, :L, :, None], pi[1][:, :L, :, None], bbr[1][:, None], bbi[1][:, None])
    win = jnp.concatenate([to_in(*wf), to_in(*wb)], axis=2)
    to_out = lambda r, i: jnp.concatenate([r, -i], axis=3).transpose(0, 3, 1, 2).reshape(G, 2 * P, L * Hg)
    of = _cmul(cr[0][:, None], ci[0][:, None], pr[0][:, 1:, None, :], pi[0][:, 1:, None, :])
    ob = _cmul(cr[1][:, None], ci[1][:, None], pr[1][:, ::-1][:, :L, None, :], pi[1][:, ::-1][:, :L, None, :])
    wout = jnp.concatenate([to_out(*of), to_out(*ob)], axis=1)
    levels = max(chunks_per_seq.bit_length() - 1, 1)
    dr, di = powers(L * (2.0 ** jnp.arange(levels, dtype=f32)))
    rows = jnp.stack([jnp.concatenate([dr, dr], -1), jnp.concatenate([-di, di], -1)], axis=3)
    dec = rows.transpose(1, 2, 0, 3, 4).reshape(G, levels * 4, 2 * P)
    return strip, win.astype(MXU_DTYPE), wout.astype(MXU_DTYPE), dec.astype(f32)


def _mix_kernel(x_ref, o_ref, gy_ref, gate_ref, woa_ref, wglu_ref, wos_ref, wout_ref, gffn_ref,
                x1_ref, h2t_ref):
    D = x_ref.shape[1]
    ya = _dot(o_ref[...], woa_ref[...])
    z = _dot(gy_ref[...], wglu_ref[...])
    yg = z[:, :SSM_WIDTH] * jax.nn.sigmoid(z[:, SSM_WIDTH:])
    ys = _dot(yg.astype(MXU_DTYPE), wos_ref[...])
    g = gate_ref[...].astype(jnp.float32)
    mixed = g[:, :D] * ya + g[:, D:] * ys
    x1 = x_ref[...] + _dot(mixed.astype(MXU_DTYPE), wout_ref[...])
    x1_ref[...] = x1
    h2t_ref[...] = _rms(x1, gffn_ref[...]).T.astype(h2t_ref.dtype)


def _mix(x2, o2, gy, gates, woa, wglu, wos, wout, gffn, tm):
    T, D = x2.shape
    row = lambda a: pl.BlockSpec((tm, a.shape[1]), lambda i: (i, 0))
    return pl.pallas_call(
        _mix_kernel,
        grid=(T // tm,),
        in_specs=[row(x2), row(o2), row(gy), row(gates),
                  _full(woa.shape), _full(wglu.shape), _full(wos.shape), _full(wout.shape), _full(gffn.shape)],
        out_specs=[pl.BlockSpec((tm, D), lambda i: (i, 0)), pl.BlockSpec((D, tm), lambda i: (0, i))],
        out_shape=[jax.ShapeDtypeStruct((T, D), jnp.float32), jax.ShapeDtypeStruct((D, T), MXU_DTYPE)],
        compiler_params=_params("parallel"),
        name="mix",
    )(x2, o2, gy, gates, woa, wglu, wos, wout, gffn)


def _extract_top(s, k, out_ref=None):
    kth = None
    rank = jnp.full(s.shape, float(k), jnp.float32)
    for i in range(k):
        kth = jnp.max(s, axis=0, keepdims=True)
        if out_ref is not None:
            out_ref[i:i + 1, :] = kth
        hit = s == kth
        rank = jnp.where(hit, float(i), rank)
        s = jnp.where(hit, -jnp.inf, s)
    return kth, rank


def _peer_stats_kernel(h2t_ref, wqt_ref, keys_ref, n1_ref, c_ref, r2_ref, e2_ref,
                       top1_ref, top2_ref, cand_ref):
    K = PEER_TOPK
    qpt = _dot(wqt_ref[...], h2t_ref[...])
    for hd in range(PEER_HEADS):
        r0 = hd * 2 * PEER_HALF
        s1 = _dot(keys_ref[2 * hd], qpt[r0:r0 + PEER_HALF].astype(MXU_DTYPE))
        s2 = _dot(keys_ref[2 * hd + 1], qpt[r0 + PEER_HALF:r0 + 2 * PEER_HALF].astype(MXU_DTYPE))
        _extract_top(s1, K, top1_ref)
        _, rank2 = _extract_top(s2, K, top2_ref)
        row = 0
        for lo, hi, keep in PEER_CAND_BLOCKS:
            for r1 in range(lo, hi):
                cand_ref[row:row + keep, :] = top1_ref[r1:r1 + 1, :] + top2_ref[0:keep, :]
                row += keep
        cand_ref[row:row + K // 2, :] = top1_ref[K // 2:K, :] + top2_ref[0:1, :]
        cand = cand_ref[...]
        tau, _ = _extract_top(cand, K)
        m1 = top1_ref[0:1, :]
        m2 = top2_ref[0:1, :]
        z = jnp.sum(jnp.where(cand >= tau, jnp.exp(cand - (m1 + m2)), 0.0), axis=0, keepdims=True)
        n1 = jnp.zeros(s1.shape, jnp.float32)
        for r in range(K // 2):
            n1 = n1 + jnp.where(s1 + top2_ref[r:r + 1, :] >= tau, 1.0, 0.0)
        extra = jnp.zeros(m1.shape, jnp.float32)
        for r in range(K // 2, K):
            extra = extra + jnp.where(m1 + top2_ref[r:r + 1, :] >= tau, 1.0, 0.0)
        n1_ref[hd] = n1 + jnp.where(s1 == m1, extra, 0.0)
        c_ref[hd] = jnp.exp(s1 - m1) * ((0.5 / GELU_C0) / z)
        r2_ref[hd] = rank2.astype(r2_ref.dtype)
        e2_ref[hd] = jnp.exp(s2 - m2).astype(e2_ref.dtype)


def _peer_stats(h2t, wqt, keys, tm):
    D, T = h2t.shape
    st_spec = pl.BlockSpec((PEER_HEADS, N_KEYS, tm), lambda i: (0, 0, i))
    st = lambda dt: jax.ShapeDtypeStruct((PEER_HEADS, N_KEYS, T), dt)
    return pl.pallas_call(
        _peer_stats_kernel,
        grid=(T // tm,),
        in_specs=[pl.BlockSpec((D, tm), lambda i: (0, i)), _full(wqt.shape), _full(keys.shape)],
        out_specs=[st_spec] * 4,
        out_shape=[st(jnp.float32), st(jnp.float32), st(MXU_DTYPE), st(MXU_DTYPE)],
        scratch_shapes=[pltpu.VMEM((PEER_TOPK, tm), jnp.float32),
                        pltpu.VMEM((PEER_TOPK, tm), jnp.float32),
                        pltpu.VMEM((PEER_CAND_ROWS, tm), jnp.float32)],
        compiler_params=_params("parallel"),
        name="peer_stats",
    )(h2t, wqt, keys)


def _gelu_scaled(a):
    return a + a * jnp.tanh(a * (1.0 + (GELU_C1 / GELU_C0 ** 2) * (a * a)))


def _peer_kernel(h2t_ref, wd_ref, wu_ref, n1_ref, c_ref, r2_ref, e2_ref, x1_ref, gfin_ref,
                 out_ref, acc_ref, act_ref):
    e = pl.program_id(1)

    @pl.when(e == 0)
    def _():
        acc_ref[...] = jnp.zeros(acc_ref.shape, jnp.float32)

    te, tm = wd_ref.shape[0], h2t_ref.shape[1]
    pk = BF16_SUBLANES
    nb = N_KEYS // pk
    blocks_per_sub = PEER_SUB // N_KEYS

    total = None
    for k in range(te // PEER_SUB):
        rows = slice(k * PEER_SUB, (k + 1) * PEER_SUB)
        a_t = _dot(wd_ref[rows, :], h2t_ref[...])
        for j in range(blocks_per_sub):
            i1 = (e * (te // PEER_SUB) + k) * blocks_per_sub + j
            gate = None
            for hd in range(PEER_HEADS):
                n1 = jnp.broadcast_to(n1_ref[hd, pl.ds(i1, 1), :], (pk, tm)).astype(MXU_DTYPE)
                cw = jnp.broadcast_to(c_ref[hd, pl.ds(i1, 1), :], (pk, tm)).astype(MXU_DTYPE)
                r2 = r2_ref[hd].reshape(nb, pk, tm)
                e2 = e2_ref[hd].reshape(nb, pk, tm)
                g = jnp.where(r2 < n1[None], e2, jnp.zeros_like(e2)) * cw[None]
                gate = g if gate is None else gate + g
            blk = a_t[j * N_KEYS:(j + 1) * N_KEYS].astype(MXU_DTYPE).reshape(nb, pk, tm)
            act = _gelu_scaled(blk) * gate
            act_ref[k * PEER_SUB + j * N_KEYS:k * PEER_SUB + (j + 1) * N_KEYS, :] = act.reshape(N_KEYS, tm)
        part = lax.dot_general(act_ref[rows, :], wu_ref[rows, :], _TN, preferred_element_type=jnp.float32)
        total = part if total is None else total + part
    acc_ref[...] += total

    @pl.when(e == pl.num_programs(1) - 1)
    def _():
        out_ref[...] = _rms(x1_ref[...] + acc_ref[...], gfin_ref[...])


def _peer(h2t, wd, wu, n1, cw, r2, e2, x1, gfin, tm, te):
    D, T = h2t.shape
    E = wd.shape[0]
    st_spec = pl.BlockSpec((PEER_HEADS, N_KEYS, tm), lambda i, e: (0, 0, i))
    return pl.pallas_call(
        _peer_kernel,
        grid=(T // tm, E // te),
        in_specs=[pl.BlockSpec((D, tm), lambda i, e: (0, i)),
                  pl.BlockSpec((te, D), lambda i, e: (e, 0)),
                  pl.BlockSpec((te, D), lambda i, e: (e, 0)),
                  st_spec, st_spec, st_spec, st_spec,
                  pl.BlockSpec((tm, D), lambda i, e: (i, 0)),
                  pl.BlockSpec(gfin.shape, lambda i, e: (0, 0))],
        out_specs=pl.BlockSpec((tm, D), lambda i, e: (i, 0)),
        out_shape=jax.ShapeDtypeStruct((T, D), jnp.float32),
        scratch_shapes=[pltpu.VMEM((tm, D), jnp.float32), pltpu.VMEM((te, tm), MXU_DTYPE)],
        compiler_params=_params("parallel", "arbitrary"),
        name="peer_experts",
    )(h2t, wd, wu, n1, cw, r2, e2, x1, gfin)


def _head_cols(w, per_head, start, width, dst):
    K = w.shape[0]
    blk = w.reshape(K, MLA_HEADS, per_head)[:, :, start:start + width]
    blk = jnp.pad(blk, ((0, 0), (0, 0), (dst, HEAD_PAD - dst - width)))
    return blk.reshape(K, MLA_HEADS * HEAD_PAD)


def _tiles(S, T):
    fit = lambda n, pref: pref if n % pref == 0 else n
    return dict(inproj=fit(S, 256), attn_q=fit(S, 512), attn_k=fit(S, 2048), mix=fit(T, 512),
                stats=fit(T, 256), peer_tokens=fit(T, 512), peer_experts=2048)


def kernel(x, norm_mix, w_in, q_a_norm, w_q_b, kv_a_norm, w_kv_b, w_o_attn, lam_re, lam_im, log_dt, b_re, b_im, c_re, c_im, d_skip, w_glu, w_o_ssm, w_out, norm_ffn, w_query, sub_keys, w_down, w_up, final_norm):
    B, S, D = x.shape
    T = B * S
    f32 = jnp.float32
    cd = MXU_DTYPE
    half = QK_ROPE_DIM // 2
    l = 0

    pos = jnp.arange(S, dtype=f32)
    inv_freq = 1.0 / (ROPE_THETA ** (jnp.arange(0, QK_ROPE_DIM, 2, dtype=f32) / QK_ROPE_DIM))
    ang = pos[:, None] * inv_freq[None, :]
    cos, sin = jnp.cos(ang), jnp.sin(ang)
    pad = HEAD_PAD - QK_HEAD_DIM
    cos_t = jnp.concatenate([jnp.ones((S, QK_NOPE_DIM), f32), cos, cos, jnp.zeros((S, pad), f32)], axis=1)
    sin_t = jnp.concatenate([jnp.zeros((S, QK_NOPE_DIM), f32), -sin, sin, jnp.zeros((S, pad), f32)], axis=1)

    w = w_in[l]
    c0, c1, c2, c3 = Q_LORA_RANK, Q_LORA_RANK + KV_LORA_RANK, Q_LORA_RANK + KV_LORA_RANK + QK_ROPE_DIM, \
        Q_LORA_RANK + KV_LORA_RANK + QK_ROPE_DIM + SSM_WIDTH
    w_kr = w[:, c1:c2]
    zl = jnp.zeros((D, QK_NOPE_DIM), f32)
    zr = jnp.zeros((D, pad), f32)
    kr_plain = jnp.concatenate([zl, w_kr, zr], axis=1)
    kr_swap = jnp.concatenate([zl, w_kr[:, half:], w_kr[:, :half], zr], axis=1)
    wcat = jnp.concatenate([w[:, :c0], w[:, c0:c1], kr_plain, kr_swap, w[:, c2:c3], w[:, c3:]], axis=1).astype(cd)

    wq = w_q_b[l]
    wq_plain = _head_cols(wq, QK_HEAD_DIM, 0, QK_HEAD_DIM, 0)
    wq_swap = (_head_cols(wq, QK_HEAD_DIM, QK_NOPE_DIM + half, half, QK_NOPE_DIM)
               + _head_cols(wq, QK_HEAD_DIM, QK_NOPE_DIM, half, QK_NOPE_DIM + half))
    wq_cat = jnp.concatenate([wq_plain, wq_swap], axis=1).astype(cd)
    wkv = w_kv_b[l]
    wkv_cat = jnp.concatenate([_head_cols(wkv, QK_NOPE_DIM + V_HEAD_DIM, 0, QK_NOPE_DIM, 0),
                               _head_cols(wkv, QK_NOPE_DIM + V_HEAD_DIM, QK_NOPE_DIM, V_HEAD_DIM, 0)],
                              axis=1).astype(cd)

    tiles = _tiles(S, T)
    q, k, v, u, gates = _inproj(x, cos_t, sin_t, norm_mix[l][None], wcat, q_a_norm[l][None], wq_cat,
                                kv_a_norm[l][None], wkv_cat, tiles["inproj"])

    o = _attention(q, k, v, tiles["attn_q"], tiles["attn_k"])

    L, G, Hg = SSM_CHUNK, SSM_GROUPS, SSM_GROUP
    ncs = S // L
    strip, win, wout_s, dec = _s5_operators(lam_re[l], lam_im[l], log_dt[l], b_re[l], b_im[l], c_re[l], c_im[l],
                                         d_skip[l], ncs)
    u_g = u.reshape(B, ncs, L, G, Hg).transpose(3, 0, 1, 2, 4).reshape(G, B * ncs, L * Hg)
    gy_g = _s5(u_g, strip, win, wout_s, dec, ncs)
    gy = gy_g.reshape(G, B, ncs, L, Hg).transpose(1, 2, 3, 0, 4).reshape(T, SSM_WIDTH)

    x1, h2t = _mix(x.reshape(T, D), o.reshape(T, MLA_HEADS * V_HEAD_DIM), gy, gates,
                  w_o_attn[l].astype(cd), w_glu[l].astype(cd), w_o_ssm[l].astype(cd), w_out[l].astype(cd),
                  norm_ffn[l][None], tiles["mix"])

    keys = sub_keys[l].reshape(PEER_HEADS * 2, N_KEYS, PEER_HALF).astype(cd)
    n1, cw, r2, e2 = _peer_stats(h2t, w_query[l].T.astype(cd), keys, tiles["stats"])
    out = _peer(h2t, (w_down[l] * GELU_C0).astype(cd), w_up[l].astype(cd), n1, cw, r2, e2, x1, final_norm[None],
                tiles["peer_tokens"], tiles["peer_experts"])
    return out.reshape(B, S, D)
```

```python
import functools
import math

import jax
import jax.numpy as jnp
from jax import lax
from jax.experimental import pallas as pl
from jax.experimental.pallas import tpu as pltpu

MLA_HEADS = 8
QK_NOPE_DIM = 64
QK_ROPE_DIM = 32
QK_HEAD_DIM = QK_NOPE_DIM + QK_ROPE_DIM
V_HEAD_DIM = 64
Q_LORA_RANK = 256
KV_LORA_RANK = 128
ROPE_THETA = 10000.0
SSM_WIDTH = 512
SSM_GROUP = 16
SSM_GROUPS = SSM_WIDTH // SSM_GROUP
SSM_STATE = 64
PEER_HEADS = 8
N_KEYS = 128
PEER_TOPK = 16
PEER_HALF = 64
EPS = 1e-6
GELU_C0 = math.sqrt(2.0 / math.pi)
GELU_C1 = 0.044715

HEAD_PAD = 128
BF16_SUBLANES = 16
PEER_SUB = 512
PEER_CAND_BLOCKS = ((0, 1, 16), (1, 4, 8), (4, 8, 4))
PEER_CAND_ROWS = sum((hi - lo) * keep for lo, hi, keep in PEER_CAND_BLOCKS) + PEER_TOPK // 2
ATTN_HEADS_PER_STEP = 8
ATTN_KEY_CHUNK = 2048
SSM_CHUNK = 64
MXU_DTYPE = jnp.bfloat16
VMEM_LIMIT_BYTES = 56 * 1024 * 1024

_NT = (((1,), (1,)), ((), ()))
_TN = (((0,), (0,)), ((), ()))


def _dot(a, b):
    return jnp.dot(a, b, preferred_element_type=jnp.float32)


def _rms(x, g):
    return x * lax.rsqrt(jnp.mean(x * x, axis=-1, keepdims=True) + EPS) * g


def _params(*sem):
    return pltpu.CompilerParams(dimension_semantics=sem, vmem_limit_bytes=VMEM_LIMIT_BYTES)


def _full(shape):
    return pl.BlockSpec(shape, lambda *_: (0,) * len(shape))


def _inproj_kernel(x_ref, cos_ref, sin_ref, gmix_ref, wcat_ref, gq_ref, wq_ref, gkv_ref, wkv_ref,
                   q_ref, k_ref, v_ref, u_ref, gate_ref):
    x = x_ref[0]
    h = _rms(x, gmix_ref[...]).astype(MXU_DTYPE)
    cos = cos_ref[...]
    sin = sin_ref[...]
    o = 0
    hq = _dot(h, wcat_ref[:, o:o + Q_LORA_RANK]); o += Q_LORA_RANK
    hkv = _dot(h, wcat_ref[:, o:o + KV_LORA_RANK]); o += KV_LORA_RANK
    kr = _dot(h, wcat_ref[:, o:o + 2 * HEAD_PAD]); o += 2 * HEAD_PAD
    u_ref[...] = _dot(h, wcat_ref[:, o:o + SSM_WIDTH]).astype(u_ref.dtype); o += SSM_WIDTH
    gate_ref[...] = jax.nn.sigmoid(_dot(h, wcat_ref[:, o:])).astype(gate_ref.dtype)

    nq = MLA_HEADS * HEAD_PAD
    hqn = _rms(hq, gq_ref[...]).astype(MXU_DTYPE)
    qq = _dot(hqn, wq_ref[...])
    hkvn = _rms(hkv, gkv_ref[...]).astype(MXU_DTYPE)
    kv = _dot(hkvn, wkv_ref[...])
    k_rope = kr[:, :HEAD_PAD] * cos + kr[:, HEAD_PAD:] * sin
    scale = QK_HEAD_DIM ** -0.5 * math.log2(math.e)
    lane = lax.broadcasted_iota(jnp.int32, (1, HEAD_PAD), 1)
    ones_pad = (lane >= V_HEAD_DIM).astype(jnp.float32)
    for hd in range(MLA_HEADS):
        sl = slice(hd * HEAD_PAD, (hd + 1) * HEAD_PAD)
        q = qq[:, sl] * cos + qq[:, nq + hd * HEAD_PAD: nq + (hd + 1) * HEAD_PAD] * sin
        q_ref[0, hd] = (q * scale).astype(q_ref.dtype)
        k_ref[0, hd] = (kv[:, sl] + k_rope).astype(k_ref.dtype)
        v_ref[0, hd] = (kv[:, nq + hd * HEAD_PAD: nq + (hd + 1) * HEAD_PAD] + ones_pad).astype(v_ref.dtype)


def _inproj(x, cos_t, sin_t, gmix, wcat, gq, wq, gkv, wkv, tm):
    B, S, D = x.shape
    H = MLA_HEADS
    ncat = wcat.shape[1]
    ngate = ncat - (Q_LORA_RANK + KV_LORA_RANK + 2 * HEAD_PAD + SSM_WIDTH)
    grid = (B, S // tm)
    hs = jax.ShapeDtypeStruct((B, H, S, HEAD_PAD), MXU_DTYPE)
    head_spec = pl.BlockSpec((1, H, tm, HEAD_PAD), lambda b, s: (b, 0, s, 0))
    return pl.pallas_call(
        _inproj_kernel,
        grid=grid,
        in_specs=[
            pl.BlockSpec((1, tm, D), lambda b, s: (b, s, 0)),
            pl.BlockSpec((tm, HEAD_PAD), lambda b, s: (s, 0)),
            pl.BlockSpec((tm, HEAD_PAD), lambda b, s: (s, 0)),
            _full(gmix.shape), _full(wcat.shape), _full(gq.shape), _full(wq.shape),
            _full(gkv.shape), _full(wkv.shape),
        ],
        out_specs=[
            head_spec, head_spec, head_spec,
            pl.BlockSpec((tm, SSM_WIDTH), lambda b, s, n=S // tm: (b * n + s, 0)),
            pl.BlockSpec((tm, ngate), lambda b, s, n=S // tm: (b * n + s, 0)),
        ],
        out_shape=[hs, hs, hs,
                   jax.ShapeDtypeStruct((B * S, SSM_WIDTH), MXU_DTYPE),
                   jax.ShapeDtypeStruct((B * S, ngate), MXU_DTYPE)],
        compiler_params=_params("parallel", "parallel"),
        name="inproj",
    )(x, cos_t, sin_t, gmix, wcat, gq, wq, gkv, wkv)


def _attn_kernel(q_ref, k_ref, v_ref, o_ref, m_ref, acc_ref):
    ki = pl.program_id(3)

    @pl.when(ki == 0)
    def _():
        m_ref[...] = jnp.full(m_ref.shape, -jnp.inf, jnp.float32)
        acc_ref[...] = jnp.zeros(acc_ref.shape, jnp.float32)

    tk = k_ref.shape[2]
    chunk = min(ATTN_KEY_CHUNK, tk)
    for j in range(ATTN_HEADS_PER_STEP):
        q = q_ref[0, j]
        m = m_ref[j]
        acc = acc_ref[j]
        for c in range(tk // chunk):
            keys = slice(c * chunk, (c + 1) * chunk)
            s = lax.dot_general(q, k_ref[0, j, keys, :], _NT, preferred_element_type=jnp.float32)
            m_new = jnp.maximum(m, jnp.max(s, axis=1, keepdims=True))
            p = jnp.exp2((s - jnp.tile(m_new, (1, chunk // HEAD_PAD))).astype(MXU_DTYPE))
            acc = jnp.exp2(m - m_new) * acc + _dot(p, v_ref[0, j, keys, :])
            m = m_new
        m_ref[j] = m
        acc_ref[j] = acc

    @pl.when(ki == pl.num_programs(3) - 1)
    def _():
        pairs = []
        for j in range(0, ATTN_HEADS_PER_STEP, 2):
            a0, a1 = acc_ref[j], acc_ref[j + 1]
            lane = lax.broadcasted_iota(jnp.int32, a0.shape, 1)
            o0 = a0 / pltpu.roll(a0, V_HEAD_DIM, 1)
            o1 = pltpu.roll(a1, V_HEAD_DIM, 1) / a1
            pairs.append(jnp.where(lane < V_HEAD_DIM, o0, o1))
        o_ref[0] = jnp.concatenate(pairs, axis=1).astype(o_ref.dtype)


def _attention(q, k, v, tq, tk):
    B, H, S, _ = q.shape
    hp = ATTN_HEADS_PER_STEP
    grid = (B, H // hp, S // tq, S // tk)
    return pl.pallas_call(
        _attn_kernel,
        grid=grid,
        in_specs=[
            pl.BlockSpec((1, hp, tq, HEAD_PAD), lambda b, h, i, j: (b, h, i, 0)),
            pl.BlockSpec((1, hp, tk, HEAD_PAD), lambda b, h, i, j: (b, h, j, 0)),
            pl.BlockSpec((1, hp, tk, HEAD_PAD), lambda b, h, i, j: (b, h, j, 0)),
        ],
        out_specs=pl.BlockSpec((1, tq, hp * V_HEAD_DIM), lambda b, h, i, j: (b, i, h)),
        out_shape=jax.ShapeDtypeStruct((B, S, H * V_HEAD_DIM), MXU_DTYPE),
        scratch_shapes=[pltpu.VMEM((hp, tq, HEAD_PAD), jnp.float32),
                        pltpu.VMEM((hp, tq, HEAD_PAD), jnp.float32)],
        compiler_params=_params("parallel", "parallel", "parallel", "arbitrary"),
        name="attention",
    )(q, k, v)


def _s5_kernel(u_ref, strip_ref, win_ref, wout_ref, dec_ref, y_ref, mt_ref, *, chunks_per_seq):
    L, Hg = SSM_CHUNK, SSM_GROUP
    for s in range(L):
        off = (L - 1 - s) * Hg
        mt_ref[s * Hg:(s + 1) * Hg, :] = strip_ref[0, :, off:off + L * Hg].astype(mt_ref.dtype)
    u = u_ref[0]
    y = _dot(u, mt_ref[...])
    loc = _dot(u, win_ref[0])
    nc = u.shape[0]
    half = SSM_STATE
    cidx = lax.broadcasted_iota(jnp.int32, (nc, 2 * half), 0) % chunks_per_seq
    xf = loc[:, :2 * half]
    xb = loc[:, 2 * half:]
    levels = chunks_per_seq.bit_length() - 1
    for lv in range(levels):
        sh = 1 << lv
        prev = jnp.where(cidx >= sh, pltpu.roll(xf, sh, 0), 0.0)
        xf = xf + prev * dec_ref[0, 4 * lv + 0:4 * lv + 1, :] \
            + pltpu.roll(prev, half, 1) * dec_ref[0, 4 * lv + 1:4 * lv + 2, :]
        nxt = jnp.where(cidx < chunks_per_seq - sh, pltpu.roll(xb, nc - sh, 0), 0.0)
        xb = xb + nxt * dec_ref[0, 4 * lv + 2:4 * lv + 3, :] \
            + pltpu.roll(nxt, half, 1) * dec_ref[0, 4 * lv + 3:4 * lv + 4, :]
    xin_f = jnp.where(cidx >= 1, pltpu.roll(xf, 1, 0), 0.0)
    xin_b = jnp.where(cidx < chunks_per_seq - 1, pltpu.roll(xb, nc - 1, 0), 0.0)
    xin = jnp.concatenate([xin_f, xin_b], axis=1).astype(MXU_DTYPE)
    y = y + _dot(xin, wout_ref[0])
    y_ref[0] = jax.nn.gelu(y).astype(y_ref.dtype)


def _s5(u_g, strip, win, wout, dec, chunks_per_seq):
    G, NC, W = u_g.shape
    kern = functools.partial(_s5_kernel, chunks_per_seq=chunks_per_seq)
    blk = lambda a: pl.BlockSpec((1,) + a.shape[1:], lambda g: (g, 0, 0))
    return pl.pallas_call(
        kern,
        grid=(G,),
        in_specs=[blk(u_g), blk(strip), blk(win), blk(wout), blk(dec)],
        out_specs=pl.BlockSpec((1, NC, W), lambda g: (g, 0, 0)),
        out_shape=jax.ShapeDtypeStruct((G, NC, W), MXU_DTYPE),
        scratch_shapes=[pltpu.VMEM((W, W), MXU_DTYPE)],
        compiler_params=_params("parallel"),
        name="s5_scan",
    )(u_g, strip, win, wout, dec)


def _cmul(ar, ai, br, bi):
    return ar * br - ai * bi, ar * bi + ai * br


def _s5_operators(lam_re, lam_im, log_dt, b_re, b_im, c_re, c_im, d_skip, chunks_per_seq):
    L, G, P, Hg = SSM_CHUNK, SSM_GROUPS, SSM_STATE, SSM_GROUP
    f32 = jnp.float32
    hi = lax.Precision.HIGHEST
    lr, li = lam_re.astype(f32), lam_im.astype(f32)
    dt = jnp.exp(log_dt.astype(f32))[..., None]
    ar, ai = lr * dt, li * dt

    def powers(n):
        mag = jnp.exp(ar[:, :, None, :] * n[None, None, :, None])
        ang = ai[:, :, None, :] * n[None, None, :, None]
        return mag * jnp.cos(ang), mag * jnp.sin(ang)

    pr, pi = powers(jnp.arange(L + 1, dtype=f32))
    nr, ni = pr[:, :, 1] - 1.0, pi[:, :, 1]
    den = lr * lr + li * li
    qr, qi = (nr * lr + ni * li) / den, (ni * lr - nr * li) / den
    bbr, bbi = _cmul(qr[..., None], qi[..., None], b_re.astype(f32), b_im.astype(f32))
    cr, ci = c_re.astype(f32), c_im.astype(f32)
    wr, wi = _cmul(cr[:, :, None], ci[:, :, None], pr[:, :, :L, None, :], pi[:, :, :L, None, :])
    kern = jnp.einsum('xgqj,xgdiq->xgjdi', jnp.concatenate([bbr, -bbi], axis=2),
                      jnp.concatenate([wr, wi], axis=-1), precision=hi)
    skip = jnp.eye(Hg, dtype=f32)[None, :, None, :] * d_skip.astype(f32).reshape(G, 1, 1, Hg)
    k0 = kern[0][:, :, :1] + kern[1][:, :, :1] + skip
    strip = jnp.concatenate([kern[1][:, :, :0:-1], k0, kern[0][:, :, 1:], jnp.zeros((G, Hg, 1, Hg), f32)], axis=2)
    strip = strip.reshape(G, Hg, 2 * L * Hg)
    to_in = lambda r, i: jnp.concatenate([r, i], axis=2).transpose(0, 1, 3, 2).reshape(G, L * Hg, 2 * P)
    wf = _cmul(pr[0][:, ::-1][:, 1:, :, None], pi[0][:, ::-1][:, 1:, :, None], bbr[0][:, None], bbi[0][:, None])
    wb = _cmul(pr[1][:, :L, :, None], pi[1][:, :L, :, None], bbr[1][:, None], bbi[1][:, None])
    win = jnp.concatenate([to_in(*wf), to_in(*wb)], axis=2)
    to_out = lambda r, i: jnp.concatenate([r, -i], axis=3).transpose(0, 3, 1, 2).reshape(G, 2 * P, L * Hg)
    of = _cmul(cr[0][:, None], ci[0][:, None], pr[0][:, 1:, None, :], pi[0][:, 1:, None, :])
    ob = _cmul(cr[1][:, None], ci[1][:, None], pr[1][:, ::-1][:, :L, None, :], pi[1][:, ::-1][:, :L, None, :])
    wout = jnp.concatenate([to_out(*of), to_out(*ob)], axis=1)
    levels = max(chunks_per_seq.bit_length() - 1, 1)
    dr, di = powers(L * (2.0 ** jnp.arange(levels, dtype=f32)))
    rows = jnp.stack([jnp.concatenate([dr, dr], -1), jnp.concatenate([-di, di], -1)], axis=3)
    dec = rows.transpose(1, 2, 0, 3, 4).reshape(G, levels * 4, 2 * P)
    return strip, win.astype(MXU_DTYPE), wout.astype(MXU_DTYPE), dec.astype(f32)


def _mix_kernel(x_ref, o_ref, gy_ref, gate_ref, woa_ref, wglu_ref, wos_ref, wout_ref, gffn_ref,
                x1_ref, h2t_ref):
    D = x_ref.shape[1]
    ya = _dot(o_ref[...], woa_ref[...])
    z = _dot(gy_ref[...], wglu_ref[...])
    yg = z[:, :SSM_WIDTH] * jax.nn.sigmoid(z[:, SSM_WIDTH:])
    ys = _dot(yg.astype(MXU_DTYPE), wos_ref[...])
    g = gate_ref[...].astype(jnp.float32)
    mixed = g[:, :D] * ya + g[:, D:] * ys
    x1 = x_ref[...] + _dot(mixed.astype(MXU_DTYPE), wout_ref[...])
    x1_ref[...] = x1
    h2t_ref[...] = _rms(x1, gffn_ref[...]).T.astype(h2t_ref.dtype)


def _mix(x2, o2, gy, gates, woa, wglu, wos, wout, gffn, tm):
    T, D = x2.shape
    row = lambda a: pl.BlockSpec((tm, a.shape[1]), lambda i: (i, 0))
    return pl.pallas_call(
        _mix_kernel,
        grid=(T // tm,),
        in_specs=[row(x2), row(o2), row(gy), row(gates),
                  _full(woa.shape), _full(wglu.shape), _full(wos.shape), _full(wout.shape), _full(gffn.shape)],
        out_specs=[pl.BlockSpec((tm, D), lambda i: (i, 0)), pl.BlockSpec((D, tm), lambda i: (0, i))],
        out_shape=[jax.ShapeDtypeStruct((T, D), jnp.float32), jax.ShapeDtypeStruct((D, T), MXU_DTYPE)],
        compiler_params=_params("parallel"),
        name="mix",
    )(x2, o2, gy, gates, woa, wglu, wos, wout, gffn)


def _extract_top(s, k, out_ref=None):
    kth = None
    rank = jnp.full(s.shape, float(k), jnp.float32)
    for i in range(k):
        kth = jnp.max(s, axis=0, keepdims=True)
        if out_ref is not None:
            out_ref[i:i + 1, :] = kth
        hit = s == kth
        rank = jnp.where(hit, float(i), rank)
        s = jnp.where(hit, -jnp.inf, s)
    return kth, rank


def _extract_top_ordered(s, k, out_ref=None):
    n = s.shape[0]
    row = lax.broadcasted_iota(jnp.int32, s.shape, 0)
    rank = jnp.full(s.shape, float(k), jnp.float32)
    for i in range(k):
        kth = jnp.max(s, axis=0, keepdims=True)
        if out_ref is not None:
            out_ref[i:i + 1, :] = kth
        hit = row == jnp.min(jnp.where(s == kth, row, n), axis=0, keepdims=True)
        rank = jnp.where(hit, float(i), rank)
        s = jnp.where(hit, -jnp.inf, s)
    return rank


def _count_at_least(s, floor):
    return jnp.sum(jnp.where(s >= floor, 1.0, 0.0), axis=0, keepdims=True)


def _peer_stats_kernel(h2t_ref, wqt_ref, keys_ref, n1_ref, c_ref, r2_ref, e2_ref,
                       qpt_ref, top1_ref, top2_ref, cand_ref, pairs_ref):
    K = PEER_TOPK
    qpt_ref[...] = _dot(wqt_ref[...], h2t_ref[...])
    surplus = None
    for hd in range(PEER_HEADS):
        r0 = hd * 2 * PEER_HALF
        s1 = _dot(keys_ref[2 * hd], qpt_ref[r0:r0 + PEER_HALF, :].astype(MXU_DTYPE))
        s2 = _dot(keys_ref[2 * hd + 1], qpt_ref[r0 + PEER_HALF:r0 + 2 * PEER_HALF, :].astype(MXU_DTYPE))
        _extract_top(s1, K, top1_ref)
        _, rank2 = _extract_top(s2, K, top2_ref)
        row = 0
        for lo, hi, keep in PEER_CAND_BLOCKS:
            for r1 in range(lo, hi):
                cand_ref[row:row + keep, :] = top1_ref[r1:r1 + 1, :] + top2_ref[0:keep, :]
                row += keep
        cand_ref[row:row + K // 2, :] = top1_ref[K // 2:K, :] + top2_ref[0:1, :]
        cand = cand_ref[...]
        tau, _ = _extract_top(cand, K)
        m1 = top1_ref[0:1, :]
        m2 = top2_ref[0:1, :]
        z = jnp.sum(jnp.where(cand >= tau, jnp.exp(cand - (m1 + m2)), 0.0), axis=0, keepdims=True)
        n1 = jnp.zeros(s1.shape, jnp.float32)
        for r in range(K // 2):
            n1 = n1 + jnp.where(s1 + top2_ref[r:r + 1, :] >= tau, 1.0, 0.0)
        extra = jnp.zeros(m1.shape, jnp.float32)
        for r in range(K // 2, K):
            extra = extra + jnp.where(m1 + top2_ref[r:r + 1, :] >= tau, 1.0, 0.0)
        n1_ref[hd] = n1 + jnp.where(s1 == m1, extra, 0.0)
        c_ref[hd] = jnp.exp(s1 - m1) * ((0.5 / GELU_C0) / z)
        r2_ref[hd] = rank2.astype(r2_ref.dtype)
        e2_ref[hd] = jnp.exp(s2 - m2).astype(e2_ref.dtype)
        over = (_count_at_least(s1, top1_ref[K - 1:K, :]) + _count_at_least(s2, top2_ref[K - 1:K, :])
                + _count_at_least(cand, tau)) - 3.0 * K
        surplus = over if surplus is None else jnp.maximum(surplus, over)

    @pl.when(jnp.max(surplus) > 0.0)
    def _():
        def head(hd, carry):
            r0 = pl.multiple_of(hd * 2 * PEER_HALF, 2 * PEER_HALF)
            s1 = _dot(keys_ref[2 * hd], qpt_ref[pl.ds(r0, PEER_HALF), :].astype(MXU_DTYPE))
            s2 = _dot(keys_ref[2 * hd + 1], qpt_ref[pl.ds(r0 + PEER_HALF, PEER_HALF), :].astype(MXU_DTYPE))
            rank1 = _extract_top_ordered(s1, K, top1_ref)
            rank2 = _extract_top_ordered(s2, K, top2_ref)
            for r1 in range(K):
                pairs_ref[r1 * K:(r1 + 1) * K, :] = top1_ref[r1:r1 + 1, :] + top2_ref[...]
            pairs = pairs_ref[...]
            chosen = _extract_top_ordered(pairs, K) < K
            m1 = top1_ref[0:1, :]
            m2 = top2_ref[0:1, :]
            z = jnp.sum(jnp.where(chosen, jnp.exp(pairs - (m1 + m2)), 0.0), axis=0, keepdims=True)
            n1 = jnp.zeros(s1.shape, jnp.float32)
            for r1 in range(K):
                taken = jnp.sum(jnp.where(chosen[r1 * K:(r1 + 1) * K], 1.0, 0.0), axis=0, keepdims=True)
                n1 = n1 + jnp.where(rank1 == float(r1), taken, 0.0)
            n1_ref[hd] = n1
            c_ref[hd] = jnp.exp(s1 - m1) * ((0.5 / GELU_C0) / z)
            r2_ref[hd] = rank2.astype(r2_ref.dtype)
            e2_ref[hd] = jnp.exp(s2 - m2).astype(e2_ref.dtype)
            return carry

        lax.fori_loop(0, PEER_HEADS, head, 0)


def _peer_stats(h2t, wqt, keys, tm):
    D, T = h2t.shape
    st_spec = pl.BlockSpec((PEER_HEADS, N_KEYS, tm), lambda i: (0, 0, i))
    st = lambda dt: jax.ShapeDtypeStruct((PEER_HEADS, N_KEYS, T), dt)
    return pl.pallas_call(
        _peer_stats_kernel,
        grid=(T // tm,),
        in_specs=[pl.BlockSpec((D, tm), lambda i: (0, i)), _full(wqt.shape), _full(keys.shape)],
        out_specs=[st_spec] * 4,
        out_shape=[st(jnp.float32), st(jnp.float32), st(MXU_DTYPE), st(MXU_DTYPE)],
        scratch_shapes=[pltpu.VMEM((wqt.shape[0], tm), jnp.float32),
                        pltpu.VMEM((PEER_TOPK, tm), jnp.float32),
                        pltpu.VMEM((PEER_TOPK, tm), jnp.float32),
                        pltpu.VMEM((PEER_CAND_ROWS, tm), jnp.float32),
                        pltpu.VMEM((PEER_TOPK * PEER_TOPK, tm), jnp.float32)],
        compiler_params=_params("parallel"),
        name="peer_stats",
    )(h2t, wqt, keys)


def _gelu_scaled(a):
    return a + a * jnp.tanh(a * (1.0 + (GELU_C1 / GELU_C0 ** 2) * (a * a)))


def _peer_kernel(h2t_ref, wd_ref, wu_ref, n1_ref, c_ref, r2_ref, e2_ref, x1_ref, gfin_ref,
                 out_ref, acc_ref, act_ref):
    e = pl.program_id(1)

    @pl.when(e == 0)
    def _():
        acc_ref[...] = jnp.zeros(acc_ref.shape, jnp.float32)

    te, tm = wd_ref.shape[0], h2t_ref.shape[1]
    pk = BF16_SUBLANES
    nb = N_KEYS // pk
    blocks_per_sub = PEER_SUB // N_KEYS

    total = None
    for k in range(te // PEER_SUB):
        rows = slice(k * PEER_SUB, (k + 1) * PEER_SUB)
        a_t = _dot(wd_ref[rows, :], h2t_ref[...])
        for j in range(blocks_per_sub):
            i1 = (e * (te // PEER_SUB) + k) * blocks_per_sub + j
            gate = None
            for hd in range(PEER_HEADS):
                n1 = jnp.broadcast_to(n1_ref[hd, pl.ds(i1, 1), :], (pk, tm)).astype(MXU_DTYPE)
                cw = jnp.broadcast_to(c_ref[hd, pl.ds(i1, 1), :], (pk, tm)).astype(MXU_DTYPE)
                r2 = r2_ref[hd].reshape(nb, pk, tm)
                e2 = e2_ref[hd].reshape(nb, pk, tm)
                g = jnp.where(r2 < n1[None], e2, jnp.zeros_like(e2)) * cw[None]
                gate = g if gate is None else gate + g
            blk = a_t[j * N_KEYS:(j + 1) * N_KEYS].astype(MXU_DTYPE).reshape(nb, pk, tm)
            act = _gelu_scaled(blk) * gate
            act_ref[k * PEER_SUB + j * N_KEYS:k * PEER_SUB + (j + 1) * N_KEYS, :] = act.reshape(N_KEYS, tm)
        part = lax.dot_general(act_ref[rows, :], wu_ref[rows, :], _TN, preferred_element_type=jnp.float32)
        total = part if total is None else total + part
    acc_ref[...] += total

    @pl.when(e == pl.num_programs(1) - 1)
    def _():
        out_ref[...] = _rms(x1_ref[...] + acc_ref[...], gfin_ref[...])


def _peer(h2t, wd, wu, n1, cw, r2, e2, x1, gfin, tm, te):
    D, T = h2t.shape
    E = wd.shape[0]
    st_spec = pl.BlockSpec((PEER_HEADS, N_KEYS, tm), lambda i, e: (0, 0, i))
    return pl.pallas_call(
        _peer_kernel,
        grid=(T // tm, E // te),
        in_specs=[pl.BlockSpec((D, tm), lambda i, e: (0, i)),
                  pl.BlockSpec((te, D), lambda i, e: (e, 0)),
                  pl.BlockSpec((te, D), lambda i, e: (e, 0)),
                  st_spec, st_spec, st_spec, st_spec,
                  pl.BlockSpec((tm, D), lambda i, e: (i, 0)),
                  pl.BlockSpec(gfin.shape, lambda i, e: (0, 0))],
        out_specs=pl.BlockSpec((tm, D), lambda i, e: (i, 0)),
        out_shape=jax.ShapeDtypeStruct((T, D), jnp.float32),
        scratch_shapes=[pltpu.VMEM((tm, D), jnp.float32), pltpu.VMEM((te, tm), MXU_DTYPE)],
        compiler_params=_params("parallel", "arbitrary"),
        name="peer_experts",
    )(h2t, wd, wu, n1, cw, r2, e2, x1, gfin)


def _head_cols(w, per_head, start, width, dst):
    K = w.shape[0]
    blk = w.reshape(K, MLA_HEADS, per_head)[:, :, start:start + width]
    blk = jnp.pad(blk, ((0, 0), (0, 0), (dst, HEAD_PAD - dst - width)))
    return blk.reshape(K, MLA_HEADS * HEAD_PAD)


def _tiles(S, T):
    fit = lambda n, pref: pref if n % pref == 0 else n
    return dict(inproj=fit(S, 256), attn_q=fit(S, 512), attn_k=fit(S, 2048), mix=fit(T, 512),
                stats=fit(T, 256), peer_tokens=fit(T, 512), peer_experts=2048)


def kernel(x, norm_mix, w_in, q_a_norm, w_q_b, kv_a_norm, w_kv_b, w_o_attn, lam_re, lam_im, log_dt, b_re, b_im, c_re, c_im, d_skip, w_glu, w_o_ssm, w_out, norm_ffn, w_query, sub_keys, w_down, w_up, final_norm):
    B, S, D = x.shape
    T = B * S
    f32 = jnp.float32
    cd = MXU_DTYPE
    half = QK_ROPE_DIM // 2
    l = 0

    pos = jnp.arange(S, dtype=f32)
    inv_freq = 1.0 / (ROPE_THETA ** (jnp.arange(0, QK_ROPE_DIM, 2, dtype=f32) / QK_ROPE_DIM))
    ang = pos[:, None] * inv_freq[None, :]
    cos, sin = jnp.cos(ang), jnp.sin(ang)
    pad = HEAD_PAD - QK_HEAD_DIM
    cos_t = jnp.concatenate([jnp.ones((S, QK_NOPE_DIM), f32), cos, cos, jnp.zeros((S, pad), f32)], axis=1)
    sin_t = jnp.concatenate([jnp.zeros((S, QK_NOPE_DIM), f32), -sin, sin, jnp.zeros((S, pad), f32)], axis=1)

    w = w_in[l]
    c0, c1, c2, c3 = Q_LORA_RANK, Q_LORA_RANK + KV_LORA_RANK, Q_LORA_RANK + KV_LORA_RANK + QK_ROPE_DIM, \
        Q_LORA_RANK + KV_LORA_RANK + QK_ROPE_DIM + SSM_WIDTH
    w_kr = w[:, c1:c2]
    zl = jnp.zeros((D, QK_NOPE_DIM), f32)
    zr = jnp.zeros((D, pad), f32)
    kr_plain = jnp.concatenate([zl, w_kr, zr], axis=1)
    kr_swap = jnp.concatenate([zl, w_kr[:, half:], w_kr[:, :half], zr], axis=1)
    wcat = jnp.concatenate([w[:, :c0], w[:, c0:c1], kr_plain, kr_swap, w[:, c2:c3], w[:, c3:]], axis=1).astype(cd)

    wq = w_q_b[l]
    wq_plain = _head_cols(wq, QK_HEAD_DIM, 0, QK_HEAD_DIM, 0)
    wq_swap = (_head_cols(wq, QK_HEAD_DIM, QK_NOPE_DIM + half, half, QK_NOPE_DIM)
               + _head_cols(wq, QK_HEAD_DIM, QK_NOPE_DIM, half, QK_NOPE_DIM + half))
    wq_cat = jnp.concatenate([wq_plain, wq_swap], axis=1).astype(cd)
    wkv = w_kv_b[l]
    wkv_cat = jnp.concatenate([_head_cols(wkv, QK_NOPE_DIM + V_HEAD_DIM, 0, QK_NOPE_DIM, 0),
                               _head_cols(wkv, QK_NOPE_DIM + V_HEAD_DIM, QK_NOPE_DIM, V_HEAD_DIM, 0)],
                              axis=1).astype(cd)

    tiles = _tiles(S, T)
    q, k, v, u, gates = _inproj(x, cos_t, sin_t, norm_mix[l][None], wcat, q_a_norm[l][None], wq_cat,
                                kv_a_norm[l][None], wkv_cat, tiles["inproj"])

    o = _attention(q, k, v, tiles["attn_q"], tiles["attn_k"])

    L, G, Hg = SSM_CHUNK, SSM_GROUPS, SSM_GROUP
    ncs = S // L
    strip, win, wout_s, dec = _s5_operators(lam_re[l], lam_im[l], log_dt[l], b_re[l], b_im[l], c_re[l], c_im[l],
                                         d_skip[l], ncs)
    u_g = u.reshape(B, ncs, L, G, Hg).transpose(3, 0, 1, 2, 4).reshape(G, B * ncs, L * Hg)
    gy_g = _s5(u_g, strip, win, wout_s, dec, ncs)
    gy = gy_g.reshape(G, B, ncs, L, Hg).transpose(1, 2, 3, 0, 4).reshape(T, SSM_WIDTH)

    x1, h2t = _mix(x.reshape(T, D), o.reshape(T, MLA_HEADS * V_HEAD_DIM), gy, gates,
                  w_o_attn[l].astype(cd), w_glu[l].astype(cd), w_o_ssm[l].astype(cd), w_out[l].astype(cd),
                  norm_ffn[l][None], tiles["mix"])

    keys = sub_keys[l].reshape(PEER_HEADS * 2, N_KEYS, PEER_HALF).astype(cd)
    n1, cw, r2, e2 = _peer_stats(h2t, w_query[l].T.astype(cd), keys, tiles["stats"])
    out = _peer(h2t, (w_down[l] * GELU_C0).astype(cd), w_up[l].astype(cd), n1, cw, r2, e2, x1, final_norm[None],
                tiles["peer_tokens"], tiles["peer_experts"])
    return out.reshape(B, S, D)
```

```python
import functools
import math

import jax
import jax.numpy as jnp
from jax import lax
from jax.experimental import pallas as pl
from jax.experimental.pallas import tpu as pltpu

MLA_HEADS = 8
QK_NOPE_DIM = 64
QK_ROPE_DIM = 32
QK_HEAD_DIM = QK_NOPE_DIM + QK_ROPE_DIM
V_HEAD_DIM = 64
Q_LORA_RANK = 256
KV_LORA_RANK = 128
ROPE_THETA = 10000.0
SSM_WIDTH = 512
SSM_GROUP = 16
SSM_GROUPS = SSM_WIDTH // SSM_GROUP
SSM_STATE = 64
PEER_HEADS = 8
N_KEYS = 128
PEER_TOPK = 16
PEER_HALF = 64
EPS = 1e-6
GELU_C0 = math.sqrt(2.0 / math.pi)
GELU_C1 = 0.044715

HEAD_PAD = 128
BF16_SUBLANES = 16
PEER_SUB = 512
PEER_CAND_BLOCKS = ((0, 1, 16), (1, 4, 8), (4, 8, 4))
PEER_CAND_ROWS = sum((hi - lo) * keep for lo, hi, keep in PEER_CAND_BLOCKS) + PEER_TOPK // 2
ATTN_HEADS_PER_STEP = 8
ATTN_KEY_CHUNK = 2048
SSM_CHUNK = 64
MXU_DTYPE = jnp.bfloat16
VMEM_LIMIT_BYTES = 56 * 1024 * 1024

_NT = (((1,), (1,)), ((), ()))
_TN = (((0,), (0,)), ((), ()))


def _dot(a, b):
    return jnp.dot(a, b, preferred_element_type=jnp.float32)


def _rms(x, g):
    return x * lax.rsqrt(jnp.mean(x * x, axis=-1, keepdims=True) + EPS) * g


def _params(*sem):
    return pltpu.CompilerParams(dimension_semantics=sem, vmem_limit_bytes=VMEM_LIMIT_BYTES)


def _full(shape):
    return pl.BlockSpec(shape, lambda *_: (0,) * len(shape))


def _inproj_kernel(x_ref, cos_ref, sin_ref, gmix_ref, wcat_ref, gq_ref, wq_ref, gkv_ref, wkv_ref,
                   q_ref, k_ref, v_ref, u_ref, gate_ref):
    x = x_ref[0]
    h = _rms(x, gmix_ref[...]).astype(MXU_DTYPE)
    cos = cos_ref[...]
    sin = sin_ref[...]
    o = 0
    hq = _dot(h, wcat_ref[:, o:o + Q_LORA_RANK]); o += Q_LORA_RANK
    hkv = _dot(h, wcat_ref[:, o:o + KV_LORA_RANK]); o += KV_LORA_RANK
    kr = _dot(h, wcat_ref[:, o:o + 2 * HEAD_PAD]); o += 2 * HEAD_PAD
    u_ref[...] = _dot(h, wcat_ref[:, o:o + SSM_WIDTH]).astype(u_ref.dtype); o += SSM_WIDTH
    gate_ref[...] = jax.nn.sigmoid(_dot(h, wcat_ref[:, o:])).astype(gate_ref.dtype)

    nq = MLA_HEADS * HEAD_PAD
    hqn = _rms(hq, gq_ref[...]).astype(MXU_DTYPE)
    qq = _dot(hqn, wq_ref[...])
    hkvn = _rms(hkv, gkv_ref[...]).astype(MXU_DTYPE)
    kv = _dot(hkvn, wkv_ref[...])
    k_rope = kr[:, :HEAD_PAD] * cos + kr[:, HEAD_PAD:] * sin
    scale = QK_HEAD_DIM ** -0.5 * math.log2(math.e)
    lane = lax.broadcasted_iota(jnp.int32, (1, HEAD_PAD), 1)
    ones_pad = (lane >= V_HEAD_DIM).astype(jnp.float32)
    for hd in range(MLA_HEADS):
        sl = slice(hd * HEAD_PAD, (hd + 1) * HEAD_PAD)
        q = qq[:, sl] * cos + qq[:, nq + hd * HEAD_PAD: nq + (hd + 1) * HEAD_PAD] * sin
        q_ref[0, hd] = (q * scale).astype(q_ref.dtype)
        k_ref[0, hd] = (kv[:, sl] + k_rope).astype(k_ref.dtype)
        v_ref[0, hd] = (kv[:, nq + hd * HEAD_PAD: nq + (hd + 1) * HEAD_PAD] + ones_pad).astype(v_ref.dtype)


def _inproj(x, cos_t, sin_t, gmix, wcat, gq, wq, gkv, wkv, tm):
    B, S, D = x.shape
    H = MLA_HEADS
    ncat = wcat.shape[1]
    ngate = ncat - (Q_LORA_RANK + KV_LORA_RANK + 2 * HEAD_PAD + SSM_WIDTH)
    grid = (B, S // tm)
    hs = jax.ShapeDtypeStruct((B, H, S, HEAD_PAD), MXU_DTYPE)
    head_spec = pl.BlockSpec((1, H, tm, HEAD_PAD), lambda b, s: (b, 0, s, 0))
    return pl.pallas_call(
        _inproj_kernel,
        grid=grid,
        in_specs=[
            pl.BlockSpec((1, tm, D), lambda b, s: (b, s, 0)),
            pl.BlockSpec((tm, HEAD_PAD), lambda b, s: (s, 0)),
            pl.BlockSpec((tm, HEAD_PAD), lambda b, s: (s, 0)),
            _full(gmix.shape), _full(wcat.shape), _full(gq.shape), _full(wq.shape),
            _full(gkv.shape), _full(wkv.shape),
        ],
        out_specs=[
            head_spec, head_spec, head_spec,
            pl.BlockSpec((tm, SSM_WIDTH), lambda b, s, n=S // tm: (b * n + s, 0)),
            pl.BlockSpec((tm, ngate), lambda b, s, n=S // tm: (b * n + s, 0)),
        ],
        out_shape=[hs, hs, hs,
                   jax.ShapeDtypeStruct((B * S, SSM_WIDTH), MXU_DTYPE),
                   jax.ShapeDtypeStruct((B * S, ngate), MXU_DTYPE)],
        compiler_params=_params("parallel", "parallel"),
        name="inproj",
    )(x, cos_t, sin_t, gmix, wcat, gq, wq, gkv, wkv)


def _attn_kernel(q_ref, k_ref, v_ref, o_ref, m_ref, acc_ref):
    ki = pl.program_id(3)

    @pl.when(ki == 0)
    def _():
        m_ref[...] = jnp.full(m_ref.shape, -jnp.inf, jnp.float32)
        acc_ref[...] = jnp.zeros(acc_ref.shape, jnp.float32)

    tk = k_ref.shape[2]
    chunk = min(ATTN_KEY_CHUNK, tk)
    for j in range(ATTN_HEADS_PER_STEP):
        q = q_ref[0, j]
        m = m_ref[j]
        acc = acc_ref[j]
        for c in range(tk // chunk):
            keys = slice(c * chunk, (c + 1) * chunk)
            s = lax.dot_general(q, k_ref[0, j, keys, :], _NT, preferred_element_type=jnp.float32)
            m_new = jnp.maximum(m, jnp.max(s, axis=1, keepdims=True))
            p = jnp.exp2((s - jnp.tile(m_new, (1, chunk // HEAD_PAD))).astype(MXU_DTYPE))
            acc = jnp.exp2(m - m_new) * acc + _dot(p, v_ref[0, j, keys, :])
            m = m_new
        m_ref[j] = m
        acc_ref[j] = acc

    @pl.when(ki == pl.num_programs(3) - 1)
    def _():
        pairs = []
        for j in range(0, ATTN_HEADS_PER_STEP, 2):
            a0, a1 = acc_ref[j], acc_ref[j + 1]
            lane = lax.broadcasted_iota(jnp.int32, a0.shape, 1)
            o0 = a0 / pltpu.roll(a0, V_HEAD_DIM, 1)
            o1 = pltpu.roll(a1, V_HEAD_DIM, 1) / a1
            pairs.append(jnp.where(lane < V_HEAD_DIM, o0, o1))
        o_ref[0] = jnp.concatenate(pairs, axis=1).astype(o_ref.dtype)


def _attention(q, k, v, tq, tk):
    B, H, S, _ = q.shape
    hp = ATTN_HEADS_PER_STEP
    grid = (B, H // hp, S // tq, S // tk)
    return pl.pallas_call(
        _attn_kernel,
        grid=grid,
        in_specs=[
            pl.BlockSpec((1, hp, tq, HEAD_PAD), lambda b, h, i, j: (b, h, i, 0)),
            pl.BlockSpec((1, hp, tk, HEAD_PAD), lambda b, h, i, j: (b, h, j, 0)),
            pl.BlockSpec((1, hp, tk, HEAD_PAD), lambda b, h, i, j: (b, h, j, 0)),
        ],
        out_specs=pl.BlockSpec((1, tq, hp * V_HEAD_DIM), lambda b, h, i, j: (b, i, h)),
        out_shape=jax.ShapeDtypeStruct((B, S, H * V_HEAD_DIM), MXU_DTYPE),
        scratch_shapes=[pltpu.VMEM((hp, tq, HEAD_PAD), jnp.float32),
                        pltpu.VMEM((hp, tq, HEAD_PAD), jnp.float32)],
        compiler_params=_params("parallel", "parallel", "parallel", "arbitrary"),
        name="attention",
    )(q, k, v)


def _s5_kernel(u_ref, strip_ref, win_ref, wout_ref, dec_ref, y_ref, mt_ref, *, chunks_per_seq):
    L, Hg = SSM_CHUNK, SSM_GROUP
    for s in range(L):
        off = (L - 1 - s) * Hg
        mt_ref[s * Hg:(s + 1) * Hg, :] = strip_ref[0, :, off:off + L * Hg].astype(mt_ref.dtype)
    u = u_ref[0]
    y = _dot(u, mt_ref[...])
    loc = _dot(u, win_ref[0])
    nc = u.shape[0]
    half = SSM_STATE
    cidx = lax.broadcasted_iota(jnp.int32, (nc, 2 * half), 0) % chunks_per_seq
    xf = loc[:, :2 * half]
    xb = loc[:, 2 * half:]
    levels = chunks_per_seq.bit_length() - 1
    for lv in range(levels):
        sh = 1 << lv
        prev = jnp.where(cidx >= sh, pltpu.roll(xf, sh, 0), 0.0)
        xf = xf + prev * dec_ref[0, 4 * lv + 0:4 * lv + 1, :] \
            + pltpu.roll(prev, half, 1) * dec_ref[0, 4 * lv + 1:4 * lv + 2, :]
        nxt = jnp.where(cidx < chunks_per_seq - sh, pltpu.roll(xb, nc - sh, 0), 0.0)
        xb = xb + nxt * dec_ref[0, 4 * lv + 2:4 * lv + 3, :] \
            + pltpu.roll(nxt, half, 1) * dec_ref[0, 4 * lv + 3:4 * lv + 4, :]
    xin_f = jnp.where(cidx >= 1, pltpu.roll(xf, 1, 0), 0.0)
    xin_b = jnp.where(cidx < chunks_per_seq - 1, pltpu.roll(xb, nc - 1, 0), 0.0)
    xin = jnp.concatenate([xin_f, xin_b], axis=1).astype(MXU_DTYPE)
    y = y + _dot(xin, wout_ref[0])
    y_ref[0] = jax.nn.gelu(y).astype(y_ref.dtype)


def _s5(u_g, strip, win, wout, dec, chunks_per_seq):
    G, NC, W = u_g.shape
    kern = functools.partial(_s5_kernel, chunks_per_seq=chunks_per_seq)
    blk = lambda a: pl.BlockSpec((1,) + a.shape[1:], lambda g: (g, 0, 0))
    return pl.pallas_call(
        kern,
        grid=(G,),
        in_specs=[blk(u_g), blk(strip), blk(win), blk(wout), blk(dec)],
        out_specs=pl.BlockSpec((1, NC, W), lambda g: (g, 0, 0)),
        out_shape=jax.ShapeDtypeStruct((G, NC, W), MXU_DTYPE),
        scratch_shapes=[pltpu.VMEM((W, W), MXU_DTYPE)],
        compiler_params=_params("parallel"),
        name="s5_scan",
    )(u_g, strip, win, wout, dec)


def _cmul(ar, ai, br, bi):
    return ar * br - ai * bi, ar * bi + ai * br


def _s5_operators(lam_re, lam_im, log_dt, b_re, b_im, c_re, c_im, d_skip, chunks_per_seq):
    L, G, P, Hg = SSM_CHUNK, SSM_GROUPS, SSM_STATE, SSM_GROUP
    f32 = jnp.float32
    hi = lax.Precision.HIGHEST
    lr, li = lam_re.astype(f32), lam_im.astype(f32)
    dt = jnp.exp(log_dt.astype(f32))[..., None]
    ar, ai = lr * dt, li * dt

    def powers(n):
        mag = jnp.exp(ar[:, :, None, :] * n[None, None, :, None])
        ang = ai[:, :, None, :] * n[None, None, :, None]
        return mag * jnp.cos(ang), mag * jnp.sin(ang)

    pr, pi = powers(jnp.arange(L + 1, dtype=f32))
    nr, ni = pr[:, :, 1] - 1.0, pi[:, :, 1]
    den = lr * lr + li * li
    qr, qi = (nr * lr + ni * li) / den, (ni * lr - nr * li) / den
    bbr, bbi = _cmul(qr[..., None], qi[..., None], b_re.astype(f32), b_im.astype(f32))
    cr, ci = c_re.astype(f32), c_im.astype(f32)
    wr, wi = _cmul(cr[:, :, None], ci[:, :, None], pr[:, :, :L, None, :], pi[:, :, :L, None, :])
    kern = jnp.einsum('xgqj,xgdiq->xgjdi', jnp.concatenate([bbr, -bbi], axis=2),
                      jnp.concatenate([wr, wi], axis=-1), precision=hi)
    skip = jnp.eye(Hg, dtype=f32)[None, :, None, :] * d_skip.astype(f32).reshape(G, 1, 1, Hg)
    k0 = kern[0][:, :, :1] + kern[1][:, :, :1] + skip
    strip = jnp.concatenate([kern[1][:, :, :0:-1], k0, kern[0][:, :, 1:], jnp.zeros((G, Hg, 1, Hg), f32)], axis=2)
    strip = strip.reshape(G, Hg, 2 * L * Hg)
    to_in = lambda r, i: jnp.concatenate([r, i], axis=2).transpose(0, 1, 3, 2).reshape(G, L * Hg, 2 * P)
    wf = _cmul(pr[0][:, ::-1][:, 1:, :, None], pi[0][:, ::-1][:, 1:, :, None], bbr[0][:, None], bbi[0][:, None])
    wb = _cmul(pr[1][:, :L, :, None], pi[1][:, :L, :, None], bbr[1][:, None], bbi[1][:, None])
    win = jnp.concatenate([to_in(*wf), to_in(*wb)], axis=2)
    to_out = lambda r, i: jnp.concatenate([r, -i], axis=3).transpose(0, 3, 1, 2).reshape(G, 2 * P, L * Hg)
    of = _cmul(cr[0][:, None], ci[0][:, None], pr[0][:, 1:, None, :], pi[0][:, 1:, None, :])
    ob = _cmul(cr[1][:, None], ci[1][:, None], pr[1][:, ::-1][:, :L, None, :], pi[1][:, ::-1][:, :L, None, :])
    wout = jnp.concatenate([to_out(*of), to_out(*ob)], axis=1)
    levels = max(chunks_per_seq.bit_length() - 1, 1)
    dr, di = powers(L * (2.0 ** jnp.arange(levels, dtype=f32)))
    rows = jnp.stack([jnp.concatenate([dr, dr], -1), jnp.concatenate([-di, di], -1)], axis=3)
    dec = rows.transpose(1, 2, 0, 3, 4).reshape(G, levels * 4, 2 * P)
    return strip, win.astype(MXU_DTYPE), wout.astype(MXU_DTYPE), dec.astype(f32)


def _mix_kernel(x_ref, o_ref, gy_ref, gate_ref, woa_ref, wglu_ref, wos_ref, wout_ref, gffn_ref,
                x1_ref, h2t_ref):
    D = x_ref.shape[1]
    ya = _dot(o_ref[...], woa_ref[...])
    z = _dot(gy_ref[...], wglu_ref[...])
    yg = z[:, :SSM_WIDTH] * jax.nn.sigmoid(z[:, SSM_WIDTH:])
    ys = _dot(yg.astype(MXU_DTYPE), wos_ref[...])
    g = gate_ref[...].astype(jnp.float32)
    mixed = g[:, :D] * ya + g[:, D:] * ys
    x1 = x_ref[...] + _dot(mixed.astype(MXU_DTYPE), wout_ref[...])
    x1_ref[...] = x1
    h2t_ref[...] = _rms(x1, gffn_ref[...]).T.astype(h2t_ref.dtype)


def _mix(x2, o2, gy, gates, woa, wglu, wos, wout, gffn, tm):
    T, D = x2.shape
    row = lambda a: pl.BlockSpec((tm, a.shape[1]), lambda i: (i, 0))
    return pl.pallas_call(
        _mix_kernel,
        grid=(T // tm,),
        in_specs=[row(x2), row(o2), row(gy), row(gates),
                  _full(woa.shape), _full(wglu.shape), _full(wos.shape), _full(wout.shape), _full(gffn.shape)],
        out_specs=[pl.BlockSpec((tm, D), lambda i: (i, 0)), pl.BlockSpec((D, tm), lambda i: (0, i))],
        out_shape=[jax.ShapeDtypeStruct((T, D), jnp.float32), jax.ShapeDtypeStruct((D, T), MXU_DTYPE)],
        compiler_params=_params("parallel"),
        name="mix",
    )(x2, o2, gy, gates, woa, wglu, wos, wout, gffn)


def _extract_top(s, k, out_ref=None):
    kth = None
    rank = jnp.full(s.shape, float(k), jnp.float32)
    for i in range(k):
        kth = jnp.max(s, axis=0, keepdims=True)
        if out_ref is not None:
            out_ref[i:i + 1, :] = kth
        hit = s == kth
        rank = jnp.where(hit, float(i), rank)
        s = jnp.where(hit, -jnp.inf, s)
    return kth, rank


def _extract_top_ordered(s, k, out_ref=None):
    n = s.shape[0]
    row = lax.broadcasted_iota(jnp.int32, s.shape, 0)
    rank = jnp.full(s.shape, float(k), jnp.float32)
    for i in range(k):
        kth = jnp.max(s, axis=0, keepdims=True)
        if out_ref is not None:
            out_ref[i:i + 1, :] = kth
        hit = row == jnp.min(jnp.where(s == kth, row, n), axis=0, keepdims=True)
        rank = jnp.where(hit, float(i), rank)
        s = jnp.where(hit, -jnp.inf, s)
    return rank


def _count_at_least(s, floor):
    return jnp.sum(jnp.where(s >= floor, 1.0, 0.0), axis=0, keepdims=True)


def _peer_stats_kernel(h2t_ref, wqt_ref, keys_ref, n1_ref, c_ref, r2_ref, e2_ref,
                       qpt_ref, top1_ref, top2_ref, cand_ref, pairs_ref, tied_ref):
    K = PEER_TOPK
    qpt_ref[...] = _dot(wqt_ref[...], h2t_ref[...])
    for hd in range(PEER_HEADS):
        r0 = hd * 2 * PEER_HALF
        s1 = _dot(keys_ref[2 * hd], qpt_ref[r0:r0 + PEER_HALF, :].astype(MXU_DTYPE))
        s2 = _dot(keys_ref[2 * hd + 1], qpt_ref[r0 + PEER_HALF:r0 + 2 * PEER_HALF, :].astype(MXU_DTYPE))
        _extract_top(s1, K, top1_ref)
        _, rank2 = _extract_top(s2, K, top2_ref)
        row = 0
        for lo, hi, keep in PEER_CAND_BLOCKS:
            for r1 in range(lo, hi):
                cand_ref[row:row + keep, :] = top1_ref[r1:r1 + 1, :] + top2_ref[0:keep, :]
                row += keep
        cand_ref[row:row + K // 2, :] = top1_ref[K // 2:K, :] + top2_ref[0:1, :]
        cand = cand_ref[...]
        tau, _ = _extract_top(cand, K)
        m1 = top1_ref[0:1, :]
        m2 = top2_ref[0:1, :]
        z = jnp.sum(jnp.where(cand >= tau, jnp.exp(cand - (m1 + m2)), 0.0), axis=0, keepdims=True)
        n1 = jnp.zeros(s1.shape, jnp.float32)
        for r in range(K // 2):
            n1 = n1 + jnp.where(s1 + top2_ref[r:r + 1, :] >= tau, 1.0, 0.0)
        extra = jnp.zeros(m1.shape, jnp.float32)
        for r in range(K // 2, K):
            extra = extra + jnp.where(m1 + top2_ref[r:r + 1, :] >= tau, 1.0, 0.0)
        n1_ref[hd] = n1 + jnp.where(s1 == m1, extra, 0.0)
        c_ref[hd] = jnp.exp(s1 - m1) * ((0.5 / GELU_C0) / z)
        r2_ref[hd] = rank2.astype(r2_ref.dtype)
        e2_ref[hd] = jnp.exp(s2 - m2).astype(e2_ref.dtype)
        over = (_count_at_least(s1, top1_ref[K - 1:K, :]) + _count_at_least(s2, top2_ref[K - 1:K, :])
                + _count_at_least(cand, tau)) - 3.0 * K
        tied_ref[hd] = jnp.max(over)

    def head(hd, carry):
        @pl.when(tied_ref[hd] > 0.0)
        def _():
            r0 = pl.multiple_of(hd * 2 * PEER_HALF, 2 * PEER_HALF)
            s1 = _dot(keys_ref[2 * hd], qpt_ref[pl.ds(r0, PEER_HALF), :].astype(MXU_DTYPE))
            s2 = _dot(keys_ref[2 * hd + 1], qpt_ref[pl.ds(r0 + PEER_HALF, PEER_HALF), :].astype(MXU_DTYPE))
            rank1 = _extract_top_ordered(s1, K, top1_ref)
            rank2 = _extract_top_ordered(s2, K, top2_ref)
            for r1 in range(K):
                pairs_ref[r1 * K:(r1 + 1) * K, :] = top1_ref[r1:r1 + 1, :] + top2_ref[...]
            pairs = pairs_ref[...]
            chosen = _extract_top_ordered(pairs, K) < K
            m1 = top1_ref[0:1, :]
            m2 = top2_ref[0:1, :]
            z = jnp.sum(jnp.where(chosen, jnp.exp(pairs - (m1 + m2)), 0.0), axis=0, keepdims=True)
            n1 = jnp.zeros(s1.shape, jnp.float32)
            for r1 in range(K):
                taken = jnp.sum(jnp.where(chosen[r1 * K:(r1 + 1) * K], 1.0, 0.0), axis=0, keepdims=True)
                n1 = n1 + jnp.where(rank1 == float(r1), taken, 0.0)
            n1_ref[hd] = n1
            c_ref[hd] = jnp.exp(s1 - m1) * ((0.5 / GELU_C0) / z)
            r2_ref[hd] = rank2.astype(r2_ref.dtype)
            e2_ref[hd] = jnp.exp(s2 - m2).astype(e2_ref.dtype)
        return carry

    lax.fori_loop(0, PEER_HEADS, head, 0)


def _peer_stats(h2t, wqt, keys, tm):
    D, T = h2t.shape
    st_spec = pl.BlockSpec((PEER_HEADS, N_KEYS, tm), lambda i: (0, 0, i))
    st = lambda dt: jax.ShapeDtypeStruct((PEER_HEADS, N_KEYS, T), dt)
    return pl.pallas_call(
        _peer_stats_kernel,
        grid=(T // tm,),
        in_specs=[pl.BlockSpec((D, tm), lambda i: (0, i)), _full(wqt.shape), _full(keys.shape)],
        out_specs=[st_spec] * 4,
        out_shape=[st(jnp.float32), st(jnp.float32), st(MXU_DTYPE), st(MXU_DTYPE)],
        scratch_shapes=[pltpu.VMEM((wqt.shape[0], tm), jnp.float32),
                        pltpu.VMEM((PEER_TOPK, tm), jnp.float32),
                        pltpu.VMEM((PEER_TOPK, tm), jnp.float32),
                        pltpu.VMEM((PEER_CAND_ROWS, tm), jnp.float32),
                        pltpu.VMEM((PEER_TOPK * PEER_TOPK, tm), jnp.float32),
                        pltpu.SMEM((PEER_HEADS,), jnp.float32)],
        compiler_params=_params("parallel"),
        name="peer_stats",
    )(h2t, wqt, keys)


def _gelu_scaled(a):
    return a + a * jnp.tanh(a * (1.0 + (GELU_C1 / GELU_C0 ** 2) * (a * a)))


def _peer_kernel(h2t_ref, wd_ref, wu_ref, n1_ref, c_ref, r2_ref, e2_ref, x1_ref, gfin_ref,
                 out_ref, acc_ref, act_ref):
    e = pl.program_id(1)

    @pl.when(e == 0)
    def _():
        acc_ref[...] = jnp.zeros(acc_ref.shape, jnp.float32)

    te, tm = wd_ref.shape[0], h2t_ref.shape[1]
    pk = BF16_SUBLANES
    nb = N_KEYS // pk
    blocks_per_sub = PEER_SUB // N_KEYS

    total = None
    for k in range(te // PEER_SUB):
        rows = slice(k * PEER_SUB, (k + 1) * PEER_SUB)
        a_t = _dot(wd_ref[rows, :], h2t_ref[...])
        for j in range(blocks_per_sub):
            i1 = (e * (te // PEER_SUB) + k) * blocks_per_sub + j
            gate = None
            for hd in range(PEER_HEADS):
                n1 = jnp.broadcast_to(n1_ref[hd, pl.ds(i1, 1), :], (pk, tm)).astype(MXU_DTYPE)
                cw = jnp.broadcast_to(c_ref[hd, pl.ds(i1, 1), :], (pk, tm)).astype(MXU_DTYPE)
                r2 = r2_ref[hd].reshape(nb, pk, tm)
                e2 = e2_ref[hd].reshape(nb, pk, tm)
                g = jnp.where(r2 < n1[None], e2, jnp.zeros_like(e2)) * cw[None]
                gate = g if gate is None else gate + g
            blk = a_t[j * N_KEYS:(j + 1) * N_KEYS].astype(MXU_DTYPE).reshape(nb, pk, tm)
            act = _gelu_scaled(blk) * gate
            act_ref[k * PEER_SUB + j * N_KEYS:k * PEER_SUB + (j + 1) * N_KEYS, :] = act.reshape(N_KEYS, tm)
        part = lax.dot_general(act_ref[rows, :], wu_ref[rows, :], _TN, preferred_element_type=jnp.float32)
        total = part if total is None else total + part
    acc_ref[...] += total

    @pl.when(e == pl.num_programs(1) - 1)
    def _():
        out_ref[...] = _rms(x1_ref[...] + acc_ref[...], gfin_ref[...])


def _peer(h2t, wd, wu, n1, cw, r2, e2, x1, gfin, tm, te):
    D, T = h2t.shape
    E = wd.shape[0]
    st_spec = pl.BlockSpec((PEER_HEADS, N_KEYS, tm), lambda i, e: (0, 0, i))
    return pl.pallas_call(
        _peer_kernel,
        grid=(T // tm, E // te),
        in_specs=[pl.BlockSpec((D, tm), lambda i, e: (0, i)),
                  pl.BlockSpec((te, D), lambda i, e: (e, 0)),
                  pl.BlockSpec((te, D), lambda i, e: (e, 0)),
                  st_spec, st_spec, st_spec, st_spec,
                  pl.BlockSpec((tm, D), lambda i, e: (i, 0)),
                  pl.BlockSpec(gfin.shape, lambda i, e: (0, 0))],
        out_specs=pl.BlockSpec((tm, D), lambda i, e: (i, 0)),
        out_shape=jax.ShapeDtypeStruct((T, D), jnp.float32),
        scratch_shapes=[pltpu.VMEM((tm, D), jnp.float32), pltpu.VMEM((te, tm), MXU_DTYPE)],
        compiler_params=_params("parallel", "arbitrary"),
        name="peer_experts",
    )(h2t, wd, wu, n1, cw, r2, e2, x1, gfin)


def _head_cols(w, per_head, start, width, dst):
    K = w.shape[0]
    blk = w.reshape(K, MLA_HEADS, per_head)[:, :, start:start + width]
    blk = jnp.pad(blk, ((0, 0), (0, 0), (dst, HEAD_PAD - dst - width)))
    return blk.reshape(K, MLA_HEADS * HEAD_PAD)


def _tiles(S, T):
    fit = lambda n, pref: pref if n % pref == 0 else n
    return dict(inproj=fit(S, 256), attn_q=fit(S, 512), attn_k=fit(S, 2048), mix=fit(T, 512),
                stats=fit(T, 256), peer_tokens=fit(T, 512), peer_experts=2048)


def kernel(x, norm_mix, w_in, q_a_norm, w_q_b, kv_a_norm, w_kv_b, w_o_attn, lam_re, lam_im, log_dt, b_re, b_im, c_re, c_im, d_skip, w_glu, w_o_ssm, w_out, norm_ffn, w_query, sub_keys, w_down, w_up, final_norm):
    B, S, D = x.shape
    T = B * S
    f32 = jnp.float32
    cd = MXU_DTYPE
    half = QK_ROPE_DIM // 2
    l = 0

    pos = jnp.arange(S, dtype=f32)
    inv_freq = 1.0 / (ROPE_THETA ** (jnp.arange(0, QK_ROPE_DIM, 2, dtype=f32) / QK_ROPE_DIM))
    ang = pos[:, None] * inv_freq[None, :]
    cos, sin = jnp.cos(ang), jnp.sin(ang)
    pad = HEAD_PAD - QK_HEAD_DIM
    cos_t = jnp.concatenate([jnp.ones((S, QK_NOPE_DIM), f32), cos, cos, jnp.zeros((S, pad), f32)], axis=1)
    sin_t = jnp.concatenate([jnp.zeros((S, QK_NOPE_DIM), f32), -sin, sin, jnp.zeros((S, pad), f32)], axis=1)

    w = w_in[l]
    c0, c1, c2, c3 = Q_LORA_RANK, Q_LORA_RANK + KV_LORA_RANK, Q_LORA_RANK + KV_LORA_RANK + QK_ROPE_DIM, \
        Q_LORA_RANK + KV_LORA_RANK + QK_ROPE_DIM + SSM_WIDTH
    w_kr = w[:, c1:c2]
    zl = jnp.zeros((D, QK_NOPE_DIM), f32)
    zr = jnp.zeros((D, pad), f32)
    kr_plain = jnp.concatenate([zl, w_kr, zr], axis=1)
    kr_swap = jnp.concatenate([zl, w_kr[:, half:], w_kr[:, :half], zr], axis=1)
    wcat = jnp.concatenate([w[:, :c0], w[:, c0:c1], kr_plain, kr_swap, w[:, c2:c3], w[:, c3:]], axis=1).astype(cd)

    wq = w_q_b[l]
    wq_plain = _head_cols(wq, QK_HEAD_DIM, 0, QK_HEAD_DIM, 0)
    wq_swap = (_head_cols(wq, QK_HEAD_DIM, QK_NOPE_DIM + half, half, QK_NOPE_DIM)
               + _head_cols(wq, QK_HEAD_DIM, QK_NOPE_DIM, half, QK_NOPE_DIM + half))
    wq_cat = jnp.concatenate([wq_plain, wq_swap], axis=1).astype(cd)
    wkv = w_kv_b[l]
    wkv_cat = jnp.concatenate([_head_cols(wkv, QK_NOPE_DIM + V_HEAD_DIM, 0, QK_NOPE_DIM, 0),
                               _head_cols(wkv, QK_NOPE_DIM + V_HEAD_DIM, QK_NOPE_DIM, V_HEAD_DIM, 0)],
                              axis=1).astype(cd)

    tiles = _tiles(S, T)
    q, k, v, u, gates = _inproj(x, cos_t, sin_t, norm_mix[l][None], wcat, q_a_norm[l][None], wq_cat,
                                kv_a_norm[l][None], wkv_cat, tiles["inproj"])

    o = _attention(q, k, v, tiles["attn_q"], tiles["attn_k"])

    L, G, Hg = SSM_CHUNK, SSM_GROUPS, SSM_GROUP
    ncs = S // L
    strip, win, wout_s, dec = _s5_operators(lam_re[l], lam_im[l], log_dt[l], b_re[l], b_im[l], c_re[l], c_im[l],
                                         d_skip[l], ncs)
    u_g = u.reshape(B, ncs, L, G, Hg).transpose(3, 0, 1, 2, 4).reshape(G, B * ncs, L * Hg)
    gy_g = _s5(u_g, strip, win, wout_s, dec, ncs)
    gy = gy_g.reshape(G, B, ncs, L, Hg).transpose(1, 2, 3, 0, 4).reshape(T, SSM_WIDTH)

    x1, h2t = _mix(x.reshape(T, D), o.reshape(T, MLA_HEADS * V_HEAD_DIM), gy, gates,
                  w_o_attn[l].astype(cd), w_glu[l].astype(cd), w_o_ssm[l].astype(cd), w_out[l].astype(cd),
                  norm_ffn[l][None], tiles["mix"])

    keys = sub_keys[l].reshape(PEER_HEADS * 2, N_KEYS, PEER_HALF).astype(cd)
    n1, cw, r2, e2 = _peer_stats(h2t, w_query[l].T.astype(cd), keys, tiles["stats"])
    out = _peer(h2t, (w_down[l] * GELU_C0).astype(cd), w_up[l].astype(cd), n1, cw, r2, e2, x1, final_norm[None],
                tiles["peer_tokens"], tiles["peer_experts"])
    return out.reshape(B, S, D)
```

```python
import functools
import math

import jax
import jax.numpy as jnp
from jax import lax
from jax.experimental import pallas as pl
from jax.experimental.pallas import tpu as pltpu

MLA_HEADS = 8
QK_NOPE_DIM = 64
QK_ROPE_DIM = 32
QK_HEAD_DIM = QK_NOPE_DIM + QK_ROPE_DIM
V_HEAD_DIM = 64
Q_LORA_RANK = 256
KV_LORA_RANK = 128
ROPE_THETA = 10000.0
SSM_WIDTH = 512
SSM_GROUP = 16
SSM_GROUPS = SSM_WIDTH // SSM_GROUP
SSM_STATE = 64
PEER_HEADS = 8
N_KEYS = 128
PEER_TOPK = 16
PEER_HALF = 64
EPS = 1e-6
GELU_C0 = math.sqrt(2.0 / math.pi)
GELU_C1 = 0.044715

HEAD_PAD = 128
BF16_SUBLANES = 16
PEER_SUB = 512
PEER_CAND_BLOCKS = ((0, 1, 16), (1, 4, 8), (4, 8, 4))
PEER_CAND_ROWS = sum((hi - lo) * keep for lo, hi, keep in PEER_CAND_BLOCKS) + PEER_TOPK // 2
ATTN_HEADS_PER_STEP = 8
ATTN_KEY_CHUNK = 2048
SSM_CHUNK = 64
MXU_DTYPE = jnp.bfloat16
VMEM_LIMIT_BYTES = 56 * 1024 * 1024

_NT = (((1,), (1,)), ((), ()))
_TN = (((0,), (0,)), ((), ()))


def _dot(a, b):
    return jnp.dot(a, b, preferred_element_type=jnp.float32)


def _rms(x, g):
    return x * lax.rsqrt(jnp.mean(x * x, axis=-1, keepdims=True) + EPS) * g


def _params(*sem):
    return pltpu.CompilerParams(dimension_semantics=sem, vmem_limit_bytes=VMEM_LIMIT_BYTES)


def _full(shape):
    return pl.BlockSpec(shape, lambda *_: (0,) * len(shape))


def _inproj_kernel(x_ref, cos_ref, sin_ref, gmix_ref, wcat_ref, gq_ref, wq_ref, gkv_ref, wkv_ref,
                   q_ref, k_ref, v_ref, u_ref, gate_ref):
    x = x_ref[0]
    h = _rms(x, gmix_ref[...]).astype(MXU_DTYPE)
    cos = cos_ref[...]
    sin = sin_ref[...]
    o = 0
    hq = _dot(h, wcat_ref[:, o:o + Q_LORA_RANK]); o += Q_LORA_RANK
    hkv = _dot(h, wcat_ref[:, o:o + KV_LORA_RANK]); o += KV_LORA_RANK
    kr = _dot(h, wcat_ref[:, o:o + 2 * HEAD_PAD]); o += 2 * HEAD_PAD
    u_ref[...] = _dot(h, wcat_ref[:, o:o + SSM_WIDTH]).astype(u_ref.dtype); o += SSM_WIDTH
    gate_ref[...] = jax.nn.sigmoid(_dot(h, wcat_ref[:, o:])).astype(gate_ref.dtype)

    nq = MLA_HEADS * HEAD_PAD
    hqn = _rms(hq, gq_ref[...]).astype(MXU_DTYPE)
    qq = _dot(hqn, wq_ref[...])
    hkvn = _rms(hkv, gkv_ref[...]).astype(MXU_DTYPE)
    kv = _dot(hkvn, wkv_ref[...])
    k_rope = kr[:, :HEAD_PAD] * cos + kr[:, HEAD_PAD:] * sin
    scale = QK_HEAD_DIM ** -0.5 * math.log2(math.e)
    lane = lax.broadcasted_iota(jnp.int32, (1, HEAD_PAD), 1)
    ones_pad = (lane >= V_HEAD_DIM).astype(jnp.float32)
    for hd in range(MLA_HEADS):
        sl = slice(hd * HEAD_PAD, (hd + 1) * HEAD_PAD)
        q = qq[:, sl] * cos + qq[:, nq + hd * HEAD_PAD: nq + (hd + 1) * HEAD_PAD] * sin
        q_ref[0, hd] = (q * scale).astype(q_ref.dtype)
        k_ref[0, hd] = (kv[:, sl] + k_rope).T.astype(k_ref.dtype)
        v_ref[0, hd] = (kv[:, nq + hd * HEAD_PAD: nq + (hd + 1) * HEAD_PAD] + ones_pad).astype(v_ref.dtype)


def _inproj(x, cos_t, sin_t, gmix, wcat, gq, wq, gkv, wkv, tm):
    B, S, D = x.shape
    H = MLA_HEADS
    ncat = wcat.shape[1]
    ngate = ncat - (Q_LORA_RANK + KV_LORA_RANK + 2 * HEAD_PAD + SSM_WIDTH)
    grid = (B, S // tm)
    hs = jax.ShapeDtypeStruct((B, H, S, HEAD_PAD), MXU_DTYPE)
    head_spec = pl.BlockSpec((1, H, tm, HEAD_PAD), lambda b, s: (b, 0, s, 0))
    return pl.pallas_call(
        _inproj_kernel,
        grid=grid,
        in_specs=[
            pl.BlockSpec((1, tm, D), lambda b, s: (b, s, 0)),
            pl.BlockSpec((tm, HEAD_PAD), lambda b, s: (s, 0)),
            pl.BlockSpec((tm, HEAD_PAD), lambda b, s: (s, 0)),
            _full(gmix.shape), _full(wcat.shape), _full(gq.shape), _full(wq.shape),
            _full(gkv.shape), _full(wkv.shape),
        ],
        out_specs=[
            head_spec, pl.BlockSpec((1, H, HEAD_PAD, tm), lambda b, s: (b, 0, 0, s)), head_spec,
            pl.BlockSpec((tm, SSM_WIDTH), lambda b, s, n=S // tm: (b * n + s, 0)),
            pl.BlockSpec((tm, ngate), lambda b, s, n=S // tm: (b * n + s, 0)),
        ],
        out_shape=[hs, jax.ShapeDtypeStruct((B, H, HEAD_PAD, S), MXU_DTYPE), hs,
                   jax.ShapeDtypeStruct((B * S, SSM_WIDTH), MXU_DTYPE),
                   jax.ShapeDtypeStruct((B * S, ngate), MXU_DTYPE)],
        compiler_params=_params("parallel", "parallel"),
        name="inproj",
    )(x, cos_t, sin_t, gmix, wcat, gq, wq, gkv, wkv)


def _attn_kernel(q_ref, k_ref, v_ref, o_ref, m_ref, acc_ref):
    ki = pl.program_id(3)

    @pl.when(ki == 0)
    def _():
        m_ref[...] = jnp.full(m_ref.shape, -jnp.inf, jnp.float32)
        acc_ref[...] = jnp.zeros(acc_ref.shape, jnp.float32)

    tk = v_ref.shape[2]
    chunk = min(ATTN_KEY_CHUNK, tk)
    for j in range(ATTN_HEADS_PER_STEP):
        q = q_ref[0, j]
        m = m_ref[j]
        acc = acc_ref[j]
        for c in range(tk // chunk):
            keys = slice(c * chunk, (c + 1) * chunk)
            s = _dot(q, k_ref[0, j, :, keys])
            m_new = jnp.maximum(m, jnp.max(s, axis=1, keepdims=True))
            p = jnp.exp2((s - jnp.tile(m_new, (1, chunk // HEAD_PAD))).astype(MXU_DTYPE))
            acc = jnp.exp2(m - m_new) * acc + _dot(p, v_ref[0, j, keys, :])
            m = m_new
        m_ref[j] = m
        acc_ref[j] = acc

    @pl.when(ki == pl.num_programs(3) - 1)
    def _():
        pairs = []
        for j in range(0, ATTN_HEADS_PER_STEP, 2):
            a0, a1 = acc_ref[j], acc_ref[j + 1]
            lane = lax.broadcasted_iota(jnp.int32, a0.shape, 1)
            o0 = a0 / pltpu.roll(a0, V_HEAD_DIM, 1)
            o1 = pltpu.roll(a1, V_HEAD_DIM, 1) / a1
            pairs.append(jnp.where(lane < V_HEAD_DIM, o0, o1))
        o_ref[0] = jnp.concatenate(pairs, axis=1).astype(o_ref.dtype)


def _attention(q, k, v, tq, tk):
    B, H, S, _ = q.shape
    hp = ATTN_HEADS_PER_STEP
    grid = (B, H // hp, S // tq, S // tk)
    return pl.pallas_call(
        _attn_kernel,
        grid=grid,
        in_specs=[
            pl.BlockSpec((1, hp, tq, HEAD_PAD), lambda b, h, i, j: (b, h, i, 0)),
            pl.BlockSpec((1, hp, HEAD_PAD, tk), lambda b, h, i, j: (b, h, 0, j)),
            pl.BlockSpec((1, hp, tk, HEAD_PAD), lambda b, h, i, j: (b, h, j, 0)),
        ],
        out_specs=pl.BlockSpec((1, tq, hp * V_HEAD_DIM), lambda b, h, i, j: (b, i, h)),
        out_shape=jax.ShapeDtypeStruct((B, S, H * V_HEAD_DIM), MXU_DTYPE),
        scratch_shapes=[pltpu.VMEM((hp, tq, HEAD_PAD), jnp.float32),
                        pltpu.VMEM((hp, tq, HEAD_PAD), jnp.float32)],
        compiler_params=_params("parallel", "parallel", "parallel", "arbitrary"),
        name="attention",
    )(q, k, v)


def _s5_kernel(u_ref, strip_ref, win_ref, wout_ref, dec_ref, y_ref, mt_ref, *, chunks_per_seq):
    L, Hg = SSM_CHUNK, SSM_GROUP
    for s in range(L):
        off = (L - 1 - s) * Hg
        mt_ref[s * Hg:(s + 1) * Hg, :] = strip_ref[0, :, off:off + L * Hg].astype(mt_ref.dtype)
    u = u_ref[0]
    y = _dot(u, mt_ref[...])
    loc = _dot(u, win_ref[0])
    nc = u.shape[0]
    half = SSM_STATE
    cidx = lax.broadcasted_iota(jnp.int32, (nc, 2 * half), 0) % chunks_per_seq
    xf = loc[:, :2 * half]
    xb = loc[:, 2 * half:]
    levels = chunks_per_seq.bit_length() - 1
    for lv in range(levels):
        sh = 1 << lv
        prev = jnp.where(cidx >= sh, pltpu.roll(xf, sh, 0), 0.0)
        xf = xf + prev * dec_ref[0, 4 * lv + 0:4 * lv + 1, :] \
            + pltpu.roll(prev, half, 1) * dec_ref[0, 4 * lv + 1:4 * lv + 2, :]
        nxt = jnp.where(cidx < chunks_per_seq - sh, pltpu.roll(xb, nc - sh, 0), 0.0)
        xb = xb + nxt * dec_ref[0, 4 * lv + 2:4 * lv + 3, :] \
            + pltpu.roll(nxt, half, 1) * dec_ref[0, 4 * lv + 3:4 * lv + 4, :]
    xin_f = jnp.where(cidx >= 1, pltpu.roll(xf, 1, 0), 0.0)
    xin_b = jnp.where(cidx < chunks_per_seq - 1, pltpu.roll(xb, nc - 1, 0), 0.0)
    xin = jnp.concatenate([xin_f, xin_b], axis=1).astype(MXU_DTYPE)
    y = y + _dot(xin, wout_ref[0])
    y_ref[0] = jax.nn.gelu(y).astype(y_ref.dtype)


def _s5(u_g, strip, win, wout, dec, chunks_per_seq):
    G, NC, W = u_g.shape
    kern = functools.partial(_s5_kernel, chunks_per_seq=chunks_per_seq)
    blk = lambda a: pl.BlockSpec((1,) + a.shape[1:], lambda g: (g, 0, 0))
    return pl.pallas_call(
        kern,
        grid=(G,),
        in_specs=[blk(u_g), blk(strip), blk(win), blk(wout), blk(dec)],
        out_specs=pl.BlockSpec((1, NC, W), lambda g: (g, 0, 0)),
        out_shape=jax.ShapeDtypeStruct((G, NC, W), MXU_DTYPE),
        scratch_shapes=[pltpu.VMEM((W, W), MXU_DTYPE)],
        compiler_params=_params("parallel"),
        name="s5_scan",
    )(u_g, strip, win, wout, dec)


def _cmul(ar, ai, br, bi):
    return ar * br - ai * bi, ar * bi + ai * br


def _s5_operators(lam_re, lam_im, log_dt, b_re, b_im, c_re, c_im, d_skip, chunks_per_seq):
    L, G, P, Hg = SSM_CHUNK, SSM_GROUPS, SSM_STATE, SSM_GROUP
    f32 = jnp.float32
    hi = lax.Precision.HIGHEST
    lr, li = lam_re.astype(f32), lam_im.astype(f32)
    dt = jnp.exp(log_dt.astype(f32))[..., None]
    ar, ai = lr * dt, li * dt

    def powers(n):
        mag = jnp.exp(ar[:, :, None, :] * n[None, None, :, None])
        ang = ai[:, :, None, :] * n[None, None, :, None]
        return mag * jnp.cos(ang), mag * jnp.sin(ang)

    pr, pi = powers(jnp.arange(L + 1, dtype=f32))
    nr, ni = pr[:, :, 1] - 1.0, pi[:, :, 1]
    den = lr * lr + li * li
    qr, qi = (nr * lr + ni * li) / den, (ni * lr - nr * li) / den
    bbr, bbi = _cmul(qr[..., None], qi[..., None], b_re.astype(f32), b_im.astype(f32))
    cr, ci = c_re.astype(f32), c_im.astype(f32)
    wr, wi = _cmul(cr[:, :, None], ci[:, :, None], pr[:, :, :L, None, :], pi[:, :, :L, None, :])
    kern = jnp.einsum('xgqj,xgdiq->xgjdi', jnp.concatenate([bbr, -bbi], axis=2),
                      jnp.concatenate([wr, wi], axis=-1), precision=hi)
    skip = jnp.eye(Hg, dtype=f32)[None, :, None, :] * d_skip.astype(f32).reshape(G, 1, 1, Hg)
    k0 = kern[0][:, :, :1] + kern[1][:, :, :1] + skip
    strip = jnp.concatenate([kern[1][:, :, :0:-1], k0, kern[0][:, :, 1:], jnp.zeros((G, Hg, 1, Hg), f32)], axis=2)
    strip = strip.reshape(G, Hg, 2 * L * Hg)
    to_in = lambda r, i: jnp.concatenate([r, i], axis=2).transpose(0, 1, 3, 2).reshape(G, L * Hg, 2 * P)
    wf = _cmul(pr[0][:, ::-1][:, 1:, :, None], pi[0][:, ::-1][:, 1:, :, None], bbr[0][:, None], bbi[0][:, None])
    wb = _cmul(pr[1][:, :L, :, None], pi[1][:, :L, :, None], bbr[1][:, None], bbi[1][:, None])
    win = jnp.concatenate([to_in(*wf), to_in(*wb)], axis=2)
    to_out = lambda r, i: jnp.concatenate([r, -i], axis=3).transpose(0, 3, 1, 2).reshape(G, 2 * P, L * Hg)
    of = _cmul(cr[0][:, None], ci[0][:, None], pr[0][:, 1:, None, :], pi[0][:, 1:, None, :])
    ob = _cmul(cr[1][:, None], ci[1][:, None], pr[1][:, ::-1][:, :L, None, :], pi[1][:, ::-1][:, :L, None, :])
    wout = jnp.concatenate([to_out(*of), to_out(*ob)], axis=1)
    levels = max(chunks_per_seq.bit_length() - 1, 1)
    dr, di = powers(L * (2.0 ** jnp.arange(levels, dtype=f32)))
    rows = jnp.stack([jnp.concatenate([dr, dr], -1), jnp.concatenate([-di, di], -1)], axis=3)
    dec = rows.transpose(1, 2, 0, 3, 4).reshape(G, levels * 4, 2 * P)
    return strip, win.astype(MXU_DTYPE), wout.astype(MXU_DTYPE), dec.astype(f32)


def _mix_kernel(x_ref, o_ref, gy_ref, gate_ref, woa_ref, wglu_ref, wos_ref, wout_ref, gffn_ref,
                x1_ref, h2t_ref):
    D = x_ref.shape[1]
    ya = _dot(o_ref[...], woa_ref[...])
    z = _dot(gy_ref[...], wglu_ref[...])
    yg = z[:, :SSM_WIDTH] * jax.nn.sigmoid(z[:, SSM_WIDTH:])
    ys = _dot(yg.astype(MXU_DTYPE), wos_ref[...])
    g = gate_ref[...].astype(jnp.float32)
    mixed = g[:, :D] * ya + g[:, D:] * ys
    x1 = x_ref[...] + _dot(mixed.astype(MXU_DTYPE), wout_ref[...])
    x1_ref[...] = x1
    h2t_ref[...] = _rms(x1, gffn_ref[...]).T.astype(h2t_ref.dtype)


def _mix(x2, o2, gy, gates, woa, wglu, wos, wout, gffn, tm):
    T, D = x2.shape
    row = lambda a: pl.BlockSpec((tm, a.shape[1]), lambda i: (i, 0))
    return pl.pallas_call(
        _mix_kernel,
        grid=(T // tm,),
        in_specs=[row(x2), row(o2), row(gy), row(gates),
                  _full(woa.shape), _full(wglu.shape), _full(wos.shape), _full(wout.shape), _full(gffn.shape)],
        out_specs=[pl.BlockSpec((tm, D), lambda i: (i, 0)), pl.BlockSpec((D, tm), lambda i: (0, i))],
        out_shape=[jax.ShapeDtypeStruct((T, D), jnp.float32), jax.ShapeDtypeStruct((D, T), MXU_DTYPE)],
        compiler_params=_params("parallel"),
        name="mix",
    )(x2, o2, gy, gates, woa, wglu, wos, wout, gffn)


def _extract_top(s, k, out_ref=None):
    kth = None
    rank = jnp.full(s.shape, float(k), jnp.float32)
    for i in range(k):
        kth = jnp.max(s, axis=0, keepdims=True)
        if out_ref is not None:
            out_ref[i:i + 1, :] = kth
        hit = s == kth
        rank = jnp.where(hit, float(i), rank)
        s = jnp.where(hit, -jnp.inf, s)
    return kth, rank


def _extract_top_ordered(s, k, out_ref=None):
    n = s.shape[0]
    row = lax.broadcasted_iota(jnp.int32, s.shape, 0)
    rank = jnp.full(s.shape, float(k), jnp.float32)
    for i in range(k):
        kth = jnp.max(s, axis=0, keepdims=True)
        if out_ref is not None:
            out_ref[i:i + 1, :] = kth
        hit = row == jnp.min(jnp.where(s == kth, row, n), axis=0, keepdims=True)
        rank = jnp.where(hit, float(i), rank)
        s = jnp.where(hit, -jnp.inf, s)
    return rank


def _count_at_least(s, floor):
    return jnp.sum(jnp.where(s >= floor, 1.0, 0.0), axis=0, keepdims=True)


def _peer_stats_kernel(h2t_ref, wqt_ref, keys_ref, n1_ref, c_ref, r2_ref, e2_ref,
                       qpt_ref, top1_ref, top2_ref, cand_ref, pairs_ref, tied_ref):
    K = PEER_TOPK
    qpt_ref[...] = _dot(wqt_ref[...], h2t_ref[...])
    for hd in range(PEER_HEADS):
        r0 = hd * 2 * PEER_HALF
        s1 = _dot(keys_ref[2 * hd], qpt_ref[r0:r0 + PEER_HALF, :].astype(MXU_DTYPE))
        s2 = _dot(keys_ref[2 * hd + 1], qpt_ref[r0 + PEER_HALF:r0 + 2 * PEER_HALF, :].astype(MXU_DTYPE))
        _extract_top(s1, K, top1_ref)
        _, rank2 = _extract_top(s2, K, top2_ref)
        row = 0
        for lo, hi, keep in PEER_CAND_BLOCKS:
            for r1 in range(lo, hi):
                cand_ref[row:row + keep, :] = top1_ref[r1:r1 + 1, :] + top2_ref[0:keep, :]
                row += keep
        cand_ref[row:row + K // 2, :] = top1_ref[K // 2:K, :] + top2_ref[0:1, :]
        cand = cand_ref[...]
        tau, _ = _extract_top(cand, K)
        m1 = top1_ref[0:1, :]
        m2 = top2_ref[0:1, :]
        z = jnp.sum(jnp.where(cand >= tau, jnp.exp(cand - (m1 + m2)), 0.0), axis=0, keepdims=True)
        n1 = jnp.zeros(s1.shape, jnp.float32)
        for r in range(K // 2):
            n1 = n1 + jnp.where(s1 + top2_ref[r:r + 1, :] >= tau, 1.0, 0.0)
        extra = jnp.zeros(m1.shape, jnp.float32)
        for r in range(K // 2, K):
            extra = extra + jnp.where(m1 + top2_ref[r:r + 1, :] >= tau, 1.0, 0.0)
        n1_ref[hd] = n1 + jnp.where(s1 == m1, extra, 0.0)
        c_ref[hd] = jnp.exp(s1 - m1) * ((0.5 / GELU_C0) / z)
        r2_ref[hd] = rank2.astype(r2_ref.dtype)
        e2_ref[hd] = jnp.exp(s2 - m2).astype(e2_ref.dtype)
        over = (_count_at_least(s1, top1_ref[K - 1:K, :]) + _count_at_least(s2, top2_ref[K - 1:K, :])
                + _count_at_least(cand, tau)) - 3.0 * K
        tied_ref[hd] = jnp.max(over)

    def head(hd, carry):
        @pl.when(tied_ref[hd] > 0.0)
        def _():
            r0 = pl.multiple_of(hd * 2 * PEER_HALF, 2 * PEER_HALF)
            s1 = _dot(keys_ref[2 * hd], qpt_ref[pl.ds(r0, PEER_HALF), :].astype(MXU_DTYPE))
            s2 = _dot(keys_ref[2 * hd + 1], qpt_ref[pl.ds(r0 + PEER_HALF, PEER_HALF), :].astype(MXU_DTYPE))
            rank1 = _extract_top_ordered(s1, K, top1_ref)
            rank2 = _extract_top_ordered(s2, K, top2_ref)
            for r1 in range(K):
                pairs_ref[r1 * K:(r1 + 1) * K, :] = top1_ref[r1:r1 + 1, :] + top2_ref[...]
            pairs = pairs_ref[...]
            chosen = _extract_top_ordered(pairs, K) < K
            m1 = top1_ref[0:1, :]
            m2 = top2_ref[0:1, :]
            z = jnp.sum(jnp.where(chosen, jnp.exp(pairs - (m1 + m2)), 0.0), axis=0, keepdims=True)
            n1 = jnp.zeros(s1.shape, jnp.float32)
            for r1 in range(K):
                taken = jnp.sum(jnp.where(chosen[r1 * K:(r1 + 1) * K], 1.0, 0.0), axis=0, keepdims=True)
                n1 = n1 + jnp.where(rank1 == float(r1), taken, 0.0)
            n1_ref[hd] = n1
            c_ref[hd] = jnp.exp(s1 - m1) * ((0.5 / GELU_C0) / z)
            r2_ref[hd] = rank2.astype(r2_ref.dtype)
            e2_ref[hd] = jnp.exp(s2 - m2).astype(e2_ref.dtype)
        return carry

    lax.fori_loop(0, PEER_HEADS, head, 0)


def _peer_stats(h2t, wqt, keys, tm):
    D, T = h2t.shape
    st_spec = pl.BlockSpec((PEER_HEADS, N_KEYS, tm), lambda i: (0, 0, i))
    st = lambda dt: jax.ShapeDtypeStruct((PEER_HEADS, N_KEYS, T), dt)
    return pl.pallas_call(
        _peer_stats_kernel,
        grid=(T // tm,),
        in_specs=[pl.BlockSpec((D, tm), lambda i: (0, i)), _full(wqt.shape), _full(keys.shape)],
        out_specs=[st_spec] * 4,
        out_shape=[st(jnp.float32), st(jnp.float32), st(MXU_DTYPE), st(MXU_DTYPE)],
        scratch_shapes=[pltpu.VMEM((wqt.shape[0], tm), jnp.float32),
                        pltpu.VMEM((PEER_TOPK, tm), jnp.float32),
                        pltpu.VMEM((PEER_TOPK, tm), jnp.float32),
                        pltpu.VMEM((PEER_CAND_ROWS, tm), jnp.float32),
                        pltpu.VMEM((PEER_TOPK * PEER_TOPK, tm), jnp.float32),
                        pltpu.SMEM((PEER_HEADS,), jnp.float32)],
        compiler_params=_params("parallel"),
        name="peer_stats",
    )(h2t, wqt, keys)


def _gelu_scaled(a):
    return a + a * jnp.tanh(a * (1.0 + (GELU_C1 / GELU_C0 ** 2) * (a * a)))


def _peer_kernel(h2t_ref, wd_ref, wu_ref, n1_ref, c_ref, r2_ref, e2_ref, x1_ref, gfin_ref,
                 out_ref, acc_ref, act_ref):
    e = pl.program_id(1)

    @pl.when(e == 0)
    def _():
        acc_ref[...] = jnp.zeros(acc_ref.shape, jnp.float32)

    te, tm = wd_ref.shape[0], h2t_ref.shape[1]
    pk = BF16_SUBLANES
    nb = N_KEYS // pk
    blocks_per_sub = PEER_SUB // N_KEYS

    total = None
    for k in range(te // PEER_SUB):
        rows = slice(k * PEER_SUB, (k + 1) * PEER_SUB)
        a_t = _dot(wd_ref[rows, :], h2t_ref[...])
        for j in range(blocks_per_sub):
            i1 = (e * (te // PEER_SUB) + k) * blocks_per_sub + j
            gate = None
            for hd in range(PEER_HEADS):
                n1 = jnp.broadcast_to(n1_ref[hd, pl.ds(i1, 1), :], (pk, tm)).astype(MXU_DTYPE)
                cw = jnp.broadcast_to(c_ref[hd, pl.ds(i1, 1), :], (pk, tm)).astype(MXU_DTYPE)
                r2 = r2_ref[hd].reshape(nb, pk, tm)
                e2 = e2_ref[hd].reshape(nb, pk, tm)
                g = jnp.where(r2 < n1[None], e2, jnp.zeros_like(e2)) * cw[None]
                gate = g if gate is None else gate + g
            blk = a_t[j * N_KEYS:(j + 1) * N_KEYS].astype(MXU_DTYPE).reshape(nb, pk, tm)
            act = _gelu_scaled(blk) * gate
            act_ref[k * PEER_SUB + j * N_KEYS:k * PEER_SUB + (j + 1) * N_KEYS, :] = act.reshape(N_KEYS, tm)
        part = lax.dot_general(act_ref[rows, :], wu_ref[rows, :], _TN, preferred_element_type=jnp.float32)
        total = part if total is None else total + part
    acc_ref[...] += total

    @pl.when(e == pl.num_programs(1) - 1)
    def _():
        out_ref[...] = _rms(x1_ref[...] + acc_ref[...], gfin_ref[...])


def _peer(h2t, wd, wu, n1, cw, r2, e2, x1, gfin, tm, te):
    D, T = h2t.shape
    E = wd.shape[0]
    st_spec = pl.BlockSpec((PEER_HEADS, N_KEYS, tm), lambda i, e: (0, 0, i))
    return pl.pallas_call(
        _peer_kernel,
        grid=(T // tm, E // te),
        in_specs=[pl.BlockSpec((D, tm), lambda i, e: (0, i)),
                  pl.BlockSpec((te, D), lambda i, e: (e, 0)),
                  pl.BlockSpec((te, D), lambda i, e: (e, 0)),
                  st_spec, st_spec, st_spec, st_spec,
                  pl.BlockSpec((tm, D), lambda i, e: (i, 0)),
                  pl.BlockSpec(gfin.shape, lambda i, e: (0, 0))],
        out_specs=pl.BlockSpec((tm, D), lambda i, e: (i, 0)),
        out_shape=jax.ShapeDtypeStruct((T, D), jnp.float32),
        scratch_shapes=[pltpu.VMEM((tm, D), jnp.float32), pltpu.VMEM((te, tm), MXU_DTYPE)],
        compiler_params=_params("parallel", "arbitrary"),
        name="peer_experts",
    )(h2t, wd, wu, n1, cw, r2, e2, x1, gfin)


def _head_cols(w, per_head, start, width, dst):
    K = w.shape[0]
    blk = w.reshape(K, MLA_HEADS, per_head)[:, :, start:start + width]
    blk = jnp.pad(blk, ((0, 0), (0, 0), (dst, HEAD_PAD - dst - width)))
    return blk.reshape(K, MLA_HEADS * HEAD_PAD)


def _tiles(S, T):
    fit = lambda n, pref: pref if n % pref == 0 else n
    return dict(inproj=fit(S, 256), attn_q=fit(S, 512), attn_k=fit(S, 2048), mix=fit(T, 512),
                stats=fit(T, 256), peer_tokens=fit(T, 512), peer_experts=2048)


def kernel(x, norm_mix, w_in, q_a_norm, w_q_b, kv_a_norm, w_kv_b, w_o_attn, lam_re, lam_im, log_dt, b_re, b_im, c_re, c_im, d_skip, w_glu, w_o_ssm, w_out, norm_ffn, w_query, sub_keys, w_down, w_up, final_norm):
    B, S, D = x.shape
    T = B * S
    f32 = jnp.float32
    cd = MXU_DTYPE
    half = QK_ROPE_DIM // 2
    l = 0

    pos = jnp.arange(S, dtype=f32)
    inv_freq = 1.0 / (ROPE_THETA ** (jnp.arange(0, QK_ROPE_DIM, 2, dtype=f32) / QK_ROPE_DIM))
    ang = pos[:, None] * inv_freq[None, :]
    cos, sin = jnp.cos(ang), jnp.sin(ang)
    pad = HEAD_PAD - QK_HEAD_DIM
    cos_t = jnp.concatenate([jnp.ones((S, QK_NOPE_DIM), f32), cos, cos, jnp.zeros((S, pad), f32)], axis=1)
    sin_t = jnp.concatenate([jnp.zeros((S, QK_NOPE_DIM), f32), -sin, sin, jnp.zeros((S, pad), f32)], axis=1)

    w = w_in[l]
    c0, c1, c2, c3 = Q_LORA_RANK, Q_LORA_RANK + KV_LORA_RANK, Q_LORA_RANK + KV_LORA_RANK + QK_ROPE_DIM, \
        Q_LORA_RANK + KV_LORA_RANK + QK_ROPE_DIM + SSM_WIDTH
    w_kr = w[:, c1:c2]
    zl = jnp.zeros((D, QK_NOPE_DIM), f32)
    zr = jnp.zeros((D, pad), f32)
    kr_plain = jnp.concatenate([zl, w_kr, zr], axis=1)
    kr_swap = jnp.concatenate([zl, w_kr[:, half:], w_kr[:, :half], zr], axis=1)
    wcat = jnp.concatenate([w[:, :c0], w[:, c0:c1], kr_plain, kr_swap, w[:, c2:c3], w[:, c3:]], axis=1).astype(cd)

    wq = w_q_b[l]
    wq_plain = _head_cols(wq, QK_HEAD_DIM, 0, QK_HEAD_DIM, 0)
    wq_swap = (_head_cols(wq, QK_HEAD_DIM, QK_NOPE_DIM + half, half, QK_NOPE_DIM)
               + _head_cols(wq, QK_HEAD_DIM, QK_NOPE_DIM, half, QK_NOPE_DIM + half))
    wq_cat = jnp.concatenate([wq_plain, wq_swap], axis=1).astype(cd)
    wkv = w_kv_b[l]
    wkv_cat = jnp.concatenate([_head_cols(wkv, QK_NOPE_DIM + V_HEAD_DIM, 0, QK_NOPE_DIM, 0),
                               _head_cols(wkv, QK_NOPE_DIM + V_HEAD_DIM, QK_NOPE_DIM, V_HEAD_DIM, 0)],
                              axis=1).astype(cd)

    tiles = _tiles(S, T)
    q, k, v, u, gates = _inproj(x, cos_t, sin_t, norm_mix[l][None], wcat, q_a_norm[l][None], wq_cat,
                                kv_a_norm[l][None], wkv_cat, tiles["inproj"])

    o = _attention(q, k, v, tiles["attn_q"], tiles["attn_k"])

    L, G, Hg = SSM_CHUNK, SSM_GROUPS, SSM_GROUP
    ncs = S // L
    strip, win, wout_s, dec = _s5_operators(lam_re[l], lam_im[l], log_dt[l], b_re[l], b_im[l], c_re[l], c_im[l],
                                         d_skip[l], ncs)
    u_g = u.reshape(B, ncs, L, G, Hg).transpose(3, 0, 1, 2, 4).reshape(G, B * ncs, L * Hg)
    gy_g = _s5(u_g, strip, win, wout_s, dec, ncs)
    gy = gy_g.reshape(G, B, ncs, L, Hg).transpose(1, 2, 3, 0, 4).reshape(T, SSM_WIDTH)

    x1, h2t = _mix(x.reshape(T, D), o.reshape(T, MLA_HEADS * V_HEAD_DIM), gy, gates,
                  w_o_attn[l].astype(cd), w_glu[l].astype(cd), w_o_ssm[l].astype(cd), w_out[l].astype(cd),
                  norm_ffn[l][None], tiles["mix"])

    keys = sub_keys[l].reshape(PEER_HEADS * 2, N_KEYS, PEER_HALF).astype(cd)
    n1, cw, r2, e2 = _peer_stats(h2t, w_query[l].T.astype(cd), keys, tiles["stats"])
    out = _peer(h2t, (w_down[l] * GELU_C0).astype(cd), w_up[l].astype(cd), n1, cw, r2, e2, x1, final_norm[None],
                tiles["peer_tokens"], tiles["peer_experts"])
    return out.reshape(B, S, D)
```
